```python
import jax, jax.numpy as jnp
from jax import lax
import numpy as np


D_MODEL = 2048
BATCH = 4
SEQ = 2048
DEPTH = 1
DEC_BATCH = 128
DEC_SEQ = 1
PAST_LEN = 16384
PAGE_SIZE = 128

D_CONV = D_MODEL
CONV_K = 31
MLSTM_HEADS = 4
D_MLSTM = 2 * D_MODEL
DH_MLSTM = D_MLSTM // MLSTM_HEADS
MLSTM_CHUNK = 64
FORGET_BIAS = 3.0
PEER_HEADS = 8
PEER_KEYS = 128
PEER_EXPERTS = PEER_KEYS * PEER_KEYS
PEER_TOPK = 16
PEER_DKEY = 256
PEER_BLOCK = 128
EPS = 1e-6
NEG_INIT = -1e30
COL_GLU = 2 * D_CONV
COL_QKV = 3 * D_MLSTM
COL_O = D_MLSTM
COL_IF = 2 * MLSTM_HEADS
COL_GATES = 2 * D_MODEL
IN_COLS = COL_GLU + COL_QKV + COL_O + COL_IF + COL_GATES

kernel_name = 'conformer_mlstm_peer_hybrid_step'


def rmsnorm(x, g):
    xf = x.astype(jnp.float32)
    y = xf * lax.rsqrt(jnp.mean(xf * xf, axis=-1, keepdims=True) + EPS)
    return y.astype(x.dtype) * g


def layernorm(x, g, b):
    xf = x.astype(jnp.float32)
    mu = jnp.mean(xf, axis=-1, keepdims=True)
    var = jnp.mean(jnp.square(xf - mu), axis=-1, keepdims=True)
    return ((xf - mu) * lax.rsqrt(var + EPS)).astype(x.dtype) * g + b


def conformer_branch(glu_in, buf, dw, db, ln_g, ln_b, w_out):
    a, gt = jnp.split(glu_in, 2, axis=-1)
    u = a * jax.nn.sigmoid(gt)
    full = jnp.concatenate([buf.astype(u.dtype), u], axis=1)
    y = lax.conv_general_dilated(full, dw[:, None, :].astype(u.dtype), window_strides=(1,), padding='VALID',
                                 dimension_numbers=('NWC', 'WIO', 'NWC'), feature_group_count=D_CONV) + db
    y = jax.nn.silu(layernorm(y, ln_g, ln_b))
    return y @ w_out, full[:, -(CONV_K - 1):]


def mlstm_chunk(carry, blk):
    C, n, m = carry
    q, k, v, li, lf = blk
    L = q.shape[2]
    b = jnp.cumsum(lf, axis=-1)
    causal = jnp.tril(jnp.ones((L, L), dtype=bool))
    d = b[..., :, None] - b[..., None, :] + li[..., None, :]
    d = jnp.where(causal, d, -jnp.inf)
    inter = b + m[..., None]
    m_t = jnp.maximum(inter, jnp.max(d, axis=-1))
    w_inter = jnp.exp(inter - m_t)
    s = jnp.einsum('bhtd,bhsd->bhts', q, k) * jnp.exp(d - m_t[..., None])
    num = w_inter[..., None] * jnp.einsum('bhtd,bhde->bhte', q, C) + jnp.einsum('bhts,bhse->bhte', s, v)
    den = w_inter * jnp.einsum('bhtd,bhd->bht', q, n) + jnp.sum(s, axis=-1)
    h = num / jnp.maximum(jnp.abs(den), jnp.exp(-m_t))[..., None]
    b_last = b[..., -1]
    g = b_last[..., None] - b + li
    m_new = jnp.maximum(b_last + m, jnp.max(g, axis=-1))
    w_s = jnp.exp(g - m_new[..., None])
    decay = jnp.exp(b_last + m - m_new)
    C_new = decay[..., None, None] * C + jnp.einsum('bhsd,bhse->bhde', k * w_s[..., None], v)
    n_new = decay[..., None] * n + jnp.einsum('bhs,bhsd->bhd', w_s, k)
    return (C_new, n_new, m_new), h


def mlstm_branch(qkv, o_pre, if_pre, if_b, state, norm_g, w_out, chunk):
    B, T, _ = qkv.shape
    f32 = jnp.float32
    q, k, v = jnp.split(qkv.astype(f32), 3, axis=-1)

    def heads(t):
        return t.reshape(B, T, MLSTM_HEADS, DH_MLSTM).transpose(0, 2, 1, 3)

    q, k, v = heads(q), heads(k) * (DH_MLSTM ** -0.5), heads(v)
    gates = if_pre.astype(f32) + if_b.astype(f32)
    li = gates[..., :MLSTM_HEADS].transpose(0, 2, 1)
    lf = jax.nn.log_sigmoid(gates[..., MLSTM_HEADS:]).transpose(0, 2, 1)
    nc = T // chunk

    def chunks(t):
        t = t.reshape(t.shape[:2] + (nc, chunk) + t.shape[3:])
        return jnp.moveaxis(t, 2, 0)

    carry, h = lax.scan(mlstm_chunk, state, (chunks(q), chunks(k), chunks(v), chunks(li), chunks(lf)))
    h = jnp.moveaxis(h, 0, 2).reshape(B, MLSTM_HEADS, T, DH_MLSTM).transpose(0, 2, 1, 3)
    h = h * lax.rsqrt(jnp.mean(h * h, axis=-1, keepdims=True) + EPS)
    h = h.reshape(B, T, D_MLSTM).astype(o_pre.dtype) * norm_g * jax.nn.sigmoid(o_pre)
    return h @ w_out, carry


def peer(h, wq, subkeys, u_tab, v_tab):
    B, T, D = h.shape
    x = h.reshape(B * T, D)
    n_tok = x.shape[0]
    q = (x @ wq).reshape(n_tok, PEER_HEADS, 2, PEER_DKEY // 2)
    s = jnp.einsum('thpd,hpkd->thpk', q.astype(jnp.float32), subkeys.astype(jnp.float32))
    sv, si = lax.top_k(s, PEER_TOPK)
    cand = sv[..., 0, :, None] + sv[..., 1, None, :]
    cand_idx = si[..., 0, :, None] * PEER_KEYS + si[..., 1, None, :]
    top_v, top_pos = lax.top_k(cand.reshape(n_tok, PEER_HEADS, PEER_TOPK * PEER_TOPK), PEER_TOPK)
    idx = jnp.take_along_axis(cand_idx.reshape(n_tok, PEER_HEADS, PEER_TOPK * PEER_TOPK), top_pos, axis=-1)
    g = jax.nn.softmax(top_v, axis=-1)
    pad = (-n_tok) % PEER_BLOCK
    xp = jnp.pad(x, ((0, pad), (0, 0))).reshape(-1, PEER_BLOCK, D)
    ip = jnp.pad(idx, ((0, pad), (0, 0), (0, 0))).reshape(-1, PEER_BLOCK, PEER_HEADS, PEER_TOPK)
    gp = jnp.pad(g, ((0, pad), (0, 0), (0, 0))).reshape(-1, PEER_BLOCK, PEER_HEADS, PEER_TOPK)

    def block(args):
        xb, ib, gb = args
        a = jnp.einsum('td,thkd->thk', xb, u_tab[ib])
        coef = (gb * jax.nn.gelu(a.astype(jnp.float32), approximate=False)).astype(xb.dtype)
        return jnp.einsum('thk,thkd->td', coef, v_tab[ib])

    out = lax.map(block, (xp, ip, gp))
    return out.reshape(-1, D)[:n_tok].reshape(B, T, D)


def hybrid_layer(x, c, conv_buf, st_c, st_n, st_m, chunk, ada_w, ada_b, norm1_g, norm2_g, w_in, mlstm_if_b,
                 conv_dw, conv_db, conv_ln_g, conv_ln_b, w_conv_out, mlstm_norm_g, w_mlstm_out, w_o,
                 peer_wq, peer_subkeys, peer_u, peer_v):
    mod = jax.nn.silu(c) @ ada_w + ada_b
    sh1, sc1, g1, sh2, sc2, g2 = [t[:, None, :] for t in jnp.split(mod, 6, axis=-1)]
    h = rmsnorm(x, norm1_g) * (1 + sc1) + sh1
    z = h @ w_in
    glu_in, qkv, o_pre, if_pre, gate_pre = jnp.split(
        z, [COL_GLU, COL_GLU + COL_QKV, COL_GLU + COL_QKV + COL_O, COL_GLU + COL_QKV + COL_O + COL_IF], axis=-1)
    conv_out, new_buf = conformer_branch(glu_in, conv_buf, conv_dw, conv_db, conv_ln_g, conv_ln_b, w_conv_out)
    m_out, (c_new, n_new, m_new) = mlstm_branch(qkv, o_pre, if_pre, mlstm_if_b, (st_c, st_n, st_m),
                                                mlstm_norm_g, w_mlstm_out, chunk)
    g_conv, g_mlstm = jnp.split(jax.nn.sigmoid(gate_pre), 2, axis=-1)
    x = x + g1 * ((g_conv * conv_out + g_mlstm * m_out) @ w_o)
    h2 = rmsnorm(x, norm2_g) * (1 + sc2) + sh2
    x = x + g2 * peer(h2, peer_wq, peer_subkeys, peer_u, peer_v)
    return x, new_buf, c_new, n_new, m_new


def setup_inputs(seed: int = 0) -> dict:
    key = jax.random.key(seed)
    ks = jax.random.split(key, 32)

    def nrm(k, shape, scale):
        return jax.random.normal(k, shape, jnp.float32) * scale

    D = D_MODEL
    return {
        'x_prompt': nrm(ks[0], (BATCH, SEQ, D), 1.0),
        'x_sample': nrm(ks[1], (DEC_BATCH, DEC_SEQ, D), 1.0),
        'state_conv': nrm(ks[2], (DEPTH, DEC_BATCH, CONV_K - 1, D_CONV), 1.0),
        'state_mlstm_c': nrm(ks[3], (DEPTH, DEC_BATCH, MLSTM_HEADS, DH_MLSTM, DH_MLSTM), 0.05),
        'state_mlstm_n': nrm(ks[4], (DEPTH, DEC_BATCH, MLSTM_HEADS, DH_MLSTM), 0.05),
        'state_mlstm_m': jax.random.uniform(ks[5], (DEPTH, DEC_BATCH, MLSTM_HEADS), jnp.float32, 0.0, 3.0),
        'c_prompt': nrm(ks[6], (BATCH, D), 1.0),
        'c_sample': nrm(ks[7], (DEC_BATCH, D), 1.0),
        'ada_w': nrm(ks[8], (DEPTH, D, 6 * D), 0.5 * D ** -0.5),
        'ada_b': nrm(ks[9], (DEPTH, 6 * D), 0.02),
        'norm1_g': 1.0 + nrm(ks[10], (DEPTH, D), 0.05),
        'norm2_g': 1.0 + nrm(ks[11], (DEPTH, D), 0.05),
        'w_in': nrm(ks[12], (DEPTH, D, IN_COLS), D ** -0.5),
        'mlstm_if_b': jnp.concatenate([nrm(ks[13], (DEPTH, MLSTM_HEADS), 0.05),
                                       FORGET_BIAS + nrm(ks[14], (DEPTH, MLSTM_HEADS), 0.5)], axis=-1),
        'conv_dw': nrm(ks[15], (DEPTH, CONV_K, D_CONV), CONV_K ** -0.5),
        'conv_db': nrm(ks[16], (DEPTH, D_CONV), 0.02),
        'conv_ln_g': 1.0 + nrm(ks[17], (DEPTH, D_CONV), 0.05),
        'conv_ln_b': nrm(ks[18], (DEPTH, D_CONV), 0.02),
        'w_conv_out': nrm(ks[19], (DEPTH, D_CONV, D), D_CONV ** -0.5),
        'mlstm_norm_g': 1.0 + nrm(ks[20], (DEPTH, D_MLSTM), 0.05),
        'w_mlstm_out': nrm(ks[21], (DEPTH, D_MLSTM, D), D_MLSTM ** -0.5),
        'w_o': nrm(ks[22], (DEPTH, D, D), D ** -0.5),
        'peer_wq': nrm(ks[23], (DEPTH, D, PEER_HEADS * PEER_DKEY), D ** -0.5),
        'peer_subkeys': nrm(ks[24], (DEPTH, PEER_HEADS, 2, PEER_KEYS, PEER_DKEY // 2), (PEER_DKEY // 2) ** -0.5),
        'peer_u': nrm(ks[25], (DEPTH, PEER_EXPERTS, D), D ** -0.5),
        'peer_v': nrm(ks[26], (DEPTH, PEER_EXPERTS, D), 0.5),
        'final_g': 1.0 + nrm(ks[27], (D,), 0.05),
    }


def reference(x_prompt, x_sample, state_conv, state_mlstm_c, state_mlstm_n, state_mlstm_m, c_prompt, c_sample,
              ada_w, ada_b, norm1_g, norm2_g, w_in, mlstm_if_b, conv_dw, conv_db, conv_ln_g, conv_ln_b,
              w_conv_out, mlstm_norm_g, w_mlstm_out, w_o, peer_wq, peer_subkeys, peer_u, peer_v, final_g):
    f32 = jnp.float32
    chunk_p = MLSTM_CHUNK if SEQ % MLSTM_CHUNK == 0 else SEQ
    xp, xs = x_prompt, x_sample
    conv_p, conv_s, cm_p, cm_s, nm_p, nm_s, mm_p, mm_s = [], [], [], [], [], [], [], []
    for l in range(DEPTH):
        w = (ada_w[l], ada_b[l], norm1_g[l], norm2_g[l], w_in[l], mlstm_if_b[l], conv_dw[l], conv_db[l],
             conv_ln_g[l], conv_ln_b[l], w_conv_out[l], mlstm_norm_g[l], w_mlstm_out[l], w_o[l],
             peer_wq[l], peer_subkeys[l], peer_u[l], peer_v[l])
        xp, bp, cp, np_, mp = hybrid_layer(
            xp, c_prompt, jnp.zeros((BATCH, CONV_K - 1, D_CONV), xp.dtype),
            jnp.zeros((BATCH, MLSTM_HEADS, DH_MLSTM, DH_MLSTM), f32),
            jnp.zeros((BATCH, MLSTM_HEADS, DH_MLSTM), f32),
            jnp.full((BATCH, MLSTM_HEADS), NEG_INIT, f32), chunk_p, *w)
        xs, bs, cs, ns, ms = hybrid_layer(
            xs, c_sample, state_conv[l], state_mlstm_c[l].astype(f32), state_mlstm_n[l].astype(f32),
            state_mlstm_m[l].astype(f32), DEC_SEQ, *w)
        conv_p.append(bp); conv_s.append(bs)
        cm_p.append(cp); cm_s.append(cs)
        nm_p.append(np_); nm_s.append(ns)
        mm_p.append(mp); mm_s.append(ms)
    y_prompt = rmsnorm(xp, final_g)
    y_sample = rmsnorm(xs, final_g)
    return (y_prompt, y_sample, jnp.stack(conv_p), jnp.stack(conv_s), jnp.stack(cm_p), jnp.stack(cm_s),
            jnp.stack(nm_p), jnp.stack(nm_s), jnp.stack(mm_p), jnp.stack(mm_s))
```

```python
import functools

import jax
import jax.numpy as jnp
from jax import lax
from jax.experimental import pallas as pl
from jax.experimental.pallas import tpu as pltpu

F32 = jnp.float32
BF16 = jnp.bfloat16
I32 = jnp.int32

EPS = 1e-6
NEG_INIT = -1e30
PEER_TOPK = 16
LANES = 128
VMEM_LIMIT = 56 * 1024 * 1024


def _cparams(*sem):
    return pltpu.CompilerParams(dimension_semantics=sem, vmem_limit_bytes=VMEM_LIMIT)


def _tile(target, *extents):
    t = min([target] + [e for e in extents if e])
    while any(e % t for e in extents):
        t -= LANES
    return t


def _sigmoid(x):
    return 1.0 / (1.0 + jnp.exp(-x))


def _log_sigmoid(x):
    return jnp.minimum(x, 0.0) - jnp.log1p(jnp.exp(-jnp.abs(x)))


def _split3(x):
    hi = x.astype(BF16)
    r1 = x - hi.astype(F32)
    mid = r1.astype(BF16)
    lo = (r1 - mid.astype(F32)).astype(BF16)
    return hi, mid, lo


def _ada_kernel(c_ref, w_ref, b_ref, o_ref):
    c = c_ref[...]
    a = (c * _sigmoid(c)).astype(BF16)
    o_ref[...] = jnp.dot(a, w_ref[...].astype(BF16), preferred_element_type=F32) + b_ref[...]


def _ada(c, w, b):
    m, d = c.shape
    n = w.shape[1]
    tn = _tile(1024, n)
    return pl.pallas_call(
        _ada_kernel,
        grid=(n // tn,),
        in_specs=[pl.BlockSpec((m, d), lambda j: (0, 0)),
                  pl.BlockSpec((d, tn), lambda j: (0, j)),
                  pl.BlockSpec((1, tn), lambda j: (0, j))],
        out_specs=pl.BlockSpec((m, tn), lambda j: (0, j)),
        out_shape=jax.ShapeDtypeStruct((m, n), F32),
        compiler_params=_cparams("arbitrary"),
        name="ada_mod",
    )(c, w, b.reshape(1, n))


def _norm_mod_kernel(x_ref, g_ref, sc_ref, sh_ref, o_ref):
    x = x_ref[...]
    y = x * lax.rsqrt(jnp.mean(x * x, axis=-1, keepdims=True) + EPS)
    o_ref[...] = (y * g_ref[...] * (1.0 + sc_ref[0]) + sh_ref[0]).astype(o_ref.dtype)


def _norm_mod(x, g, mod3, sh_idx, sc_idx, grp_rows, tm):
    t, d = x.shape
    r = mod3.shape[1]
    per = grp_rows // tm
    return pl.pallas_call(
        _norm_mod_kernel,
        grid=(t // tm,),
        in_specs=[pl.BlockSpec((tm, d), lambda i: (i, 0)),
                  pl.BlockSpec((1, d), lambda i: (0, 0)),
                  pl.BlockSpec((1, r, d), lambda i: (i // per, 0, sc_idx)),
                  pl.BlockSpec((1, r, d), lambda i: (i // per, 0, sh_idx))],
        out_specs=pl.BlockSpec((tm, d), lambda i: (i, 0)),
        out_shape=jax.ShapeDtypeStruct((t, d), BF16),
        compiler_params=_cparams("arbitrary"),
        name="norm_mod",
    )(x, g.reshape(1, d), mod3, mod3)


def _mm_kernel(a_ref, w_ref, o_ref, wb_ref, *, act):
    @pl.when(pl.program_id(1) == 0)
    def _():
        wb_ref[...] = w_ref[...].astype(BF16)

    acc = jnp.dot(a_ref[...], wb_ref[...], preferred_element_type=F32)
    if act == "sigmoid":
        acc = _sigmoid(acc)
    o_ref[...] = acc.astype(o_ref.dtype)


def _mm(a, w, *, col0, ncols, tm, tn, out_dtype, act=None, name="mm"):
    m, k = a.shape
    tm = min(tm, m)
    tn = _tile(tn, ncols, col0)
    cb = col0 // tn
    return pl.pallas_call(
        functools.partial(_mm_kernel, act=act),
        grid=(ncols // tn, m // tm),
        in_specs=[pl.BlockSpec((tm, k), lambda j, i: (i, 0)),
                  pl.BlockSpec((k, tn), lambda j, i: (0, cb + j))],
        out_specs=pl.BlockSpec((tm, tn), lambda j, i: (i, j)),
        out_shape=jax.ShapeDtypeStruct((m, ncols), out_dtype),
        scratch_shapes=[pltpu.VMEM((k, tn), BF16)],
        compiler_params=_cparams("arbitrary", "arbitrary"),
        name=name,
    )(a, w)


def _mm_glu_kernel(a_ref, wa_ref, wg_ref, o_ref, wab_ref, wgb_ref):
    @pl.when(pl.program_id(1) == 0)
    def _():
        wab_ref[...] = wa_ref[...].astype(BF16)
        wgb_ref[...] = wg_ref[...].astype(BF16)

    a = a_ref[...]
    lin = jnp.dot(a, wab_ref[...], preferred_element_type=F32)
    gate = jnp.dot(a, wgb_ref[...], preferred_element_type=F32)
    o_ref[...] = lin * _sigmoid(gate)


def _mm_glu(a, w, *, d_half, tm, tn):
    m, k = a.shape
    tm = min(tm, m)
    tn = _tile(tn, d_half)
    off = d_half // tn
    return pl.pallas_call(
        _mm_glu_kernel,
        grid=(d_half // tn, m // tm),
        in_specs=[pl.BlockSpec((tm, k), lambda j, i: (i, 0)),
                  pl.BlockSpec((k, tn), lambda j, i: (0, j)),
                  pl.BlockSpec((k, tn), lambda j, i: (0, off + j))],
        out_specs=pl.BlockSpec((tm, tn), lambda j, i: (i, j)),
        out_shape=jax.ShapeDtypeStruct((m, d_half), F32),
        scratch_shapes=[pltpu.VMEM((k, tn), BF16), pltpu.VMEM((k, tn), BF16)],
        compiler_params=_cparams("arbitrary", "arbitrary"),
        name="mm_glu",
    )(a, w, w)


def _mm_merge_kernel(gc_ref, co_ref, gm_ref, mo_ref, w_ref, x_ref, g_ref, o_ref, mix_ref):
    @pl.when(pl.program_id(1) == 0)
    def _():
        mix_ref[...] = (gc_ref[...] * co_ref[...] + gm_ref[...] * mo_ref[...]).astype(BF16)

    acc = jnp.dot(mix_ref[...], w_ref[...].astype(BF16), preferred_element_type=F32)
    o_ref[...] = x_ref[...] + g_ref[0] * acc


def _mm_merge(gates, conv_out, m_out, w_o, x, mod3, g_idx, grp_rows, tm, tn):
    m, d = x.shape
    tm = min(tm, m)
    tn = _tile(tn, d)
    r = mod3.shape[1]
    per = grp_rows // tm
    nj = d // tn
    return pl.pallas_call(
        _mm_merge_kernel,
        grid=(m // tm, nj),
        in_specs=[pl.BlockSpec((tm, d), lambda i, j: (i, 0)),
                  pl.BlockSpec((tm, d), lambda i, j: (i, 0)),
                  pl.BlockSpec((tm, d), lambda i, j: (i, 1)),
                  pl.BlockSpec((tm, d), lambda i, j: (i, 0)),
                  pl.BlockSpec((d, tn), lambda i, j: (0, j)),
                  pl.BlockSpec((tm, tn), lambda i, j: (i, j)),
                  pl.BlockSpec((1, r, tn), lambda i, j: (i // per, 0, g_idx * nj + j))],
        out_specs=pl.BlockSpec((tm, tn), lambda i, j: (i, j)),
        out_shape=jax.ShapeDtypeStruct((m, d), F32),
        scratch_shapes=[pltpu.VMEM((tm, d), BF16)],
        compiler_params=_cparams("arbitrary", "arbitrary"),
        name="mm_merge",
    )(gates, conv_out, gates, m_out, w_o, x, mod3)


_HALO = 32
_CONV_RC = 64


def _ln_silu(y, lg, lb):
    mu = jnp.mean(y, axis=-1, keepdims=True)
    yc = y - mu
    var = jnp.mean(yc * yc, axis=-1, keepdims=True)
    z = yc * lax.rsqrt(var + EPS) * lg + lb
    return z * _sigmoid(z)


def _conv_prompt_kernel(u_ref, dw_ref, db_ref, lg_ref, lb_ref, o_ref, win_ref, y_ref, *, tt, kw):
    d = u_ref.shape[2]
    t = pl.program_id(1)

    @pl.when(t == 0)
    def _():
        win_ref[0:_HALO, :] = jnp.zeros((_HALO, d), F32)

    @pl.when(t > 0)
    def _():
        win_ref[0:_HALO, :] = win_ref[tt:tt + _HALO, :]

    win_ref[_HALO:_HALO + tt, :] = u_ref[0]
    base = _HALO - (kw - 1)
    rc = min(_CONV_RC, tt)

    def row_chunk(r, carry):
        r0 = pl.multiple_of(r * rc, rc)
        for c in range(d // LANES):
            ls = slice(c * LANES, (c + 1) * LANES)
            w = win_ref[pl.ds(r0, rc + _HALO), ls]
            acc = jnp.broadcast_to(db_ref[:, ls], (rc, LANES))
            for j in range(kw):
                acc = acc + w[base + j:base + j + rc, :] * dw_ref[j:j + 1, ls]
            y_ref[pl.ds(r0, rc), ls] = acc
        return carry

    lax.fori_loop(0, tt // rc, row_chunk, 0)
    o_ref[0] = _ln_silu(y_ref[...], lg_ref[...], lb_ref[...]).astype(o_ref.dtype)


def _conv_prompt(u3, dw, db, lg, lb, tt):
    b, t, d = u3.shape
    tt = min(tt, t)
    kw = dw.shape[0]
    assert kw - 1 <= _HALO <= tt
    return pl.pallas_call(
        functools.partial(_conv_prompt_kernel, tt=tt, kw=kw),
        grid=(b, t // tt),
        in_specs=[pl.BlockSpec((1, tt, d), lambda i, j: (i, j, 0)),
                  pl.BlockSpec((kw, d), lambda i, j: (0, 0)),
                  pl.BlockSpec((1, d), lambda i, j: (0, 0)),
                  pl.BlockSpec((1, d), lambda i, j: (0, 0)),
                  pl.BlockSpec((1, d), lambda i, j: (0, 0))],
        out_specs=pl.BlockSpec((1, tt, d), lambda i, j: (i, j, 0)),
        out_shape=jax.ShapeDtypeStruct((b, t, d), BF16),
        scratch_shapes=[pltpu.VMEM((_HALO + tt, d), F32), pltpu.VMEM((tt, d), F32)],
        compiler_params=_cparams("arbitrary", "arbitrary"),
        name="conv_prompt",
    )(u3, dw, db.reshape(1, d), lg.reshape(1, d), lb.reshape(1, d))


def _conv_sample_kernel(buf_ref, u_ref, dw_ref, db_ref, lg_ref, lb_ref, o_ref, nb_ref, *, kw):
    u = u_ref[:, 0, :]
    acc = u * dw_ref[kw - 1:kw, :] + db_ref[...]
    for j in range(kw - 1):
        acc = acc + buf_ref[:, j, :] * dw_ref[j:j + 1, :]
    o_ref[:, 0, :] = _ln_silu(acc, lg_ref[...], lb_ref[...]).astype(o_ref.dtype)
    for j in range(kw - 2):
        nb_ref[:, j, :] = buf_ref[:, j + 1, :]
    nb_ref[:, kw - 2, :] = u


def _conv_sample(buf, u3, dw, db, lg, lb, bb=8):
    b, _, d = u3.shape
    kw = dw.shape[0]
    bb = min(bb, b)
    return pl.pallas_call(
        functools.partial(_conv_sample_kernel, kw=kw),
        grid=(b // bb,),
        in_specs=[pl.BlockSpec((bb, kw - 1, d), lambda i: (i, 0, 0)),
                  pl.BlockSpec((bb, 1, d), lambda i: (i, 0, 0)),
                  pl.BlockSpec((kw, d), lambda i: (0, 0)),
                  pl.BlockSpec((1, d), lambda i: (0, 0)),
                  pl.BlockSpec((1, d), lambda i: (0, 0)),
                  pl.BlockSpec((1, d), lambda i: (0, 0))],
        out_specs=[pl.BlockSpec((bb, 1, d), lambda i: (i, 0, 0)),
                   pl.BlockSpec((bb, kw - 1, d), lambda i: (i, 0, 0))],
        out_shape=[jax.ShapeDtypeStruct((b, 1, d), BF16),
                   jax.ShapeDtypeStruct((b, kw - 1, d), F32)],
        compiler_params=_cparams("arbitrary"),
        name="conv_sample",
    )(buf, u3, dw, db.reshape(1, d), lg.reshape(1, d), lb.reshape(1, d))


_MLSTM_CB = 256


def _mlstm_prompt_kernel(bias_ref, q_ref, k_ref, v_ref, ifc_ref, ifr_ref, og_ref, ng_ref,
                         h_ref, c_ref, n_ref, m_ref, cb_ref, ns_ref, ms_ref, *, n_heads):
    hd = pl.program_id(1)
    ck = pl.program_id(2)
    L, dk = q_ref.shape
    dv = v_ref.shape[1]
    scale = dk ** -0.5

    @pl.when(ck == 0)
    def _():
        c_ref[...] = jnp.zeros(c_ref.shape, F32)
        cb_ref[...] = jnp.zeros(cb_ref.shape, BF16)
        ns_ref[...] = jnp.zeros(ns_ref.shape, F32)
        ms_ref[...] = jnp.full(ms_ref.shape, NEG_INIT, F32)

    q = q_ref[...]
    k = k_ref[...]
    v = v_ref[...]
    bi = bias_ref[hd]
    bf = bias_ref[n_heads + hd]
    col = ifc_ref[0, 0]
    li_c = col[:, 0:1] + bi
    lf_c = _log_sigmoid(col[:, 1:2] + bf)
    row = ifr_ref[0, 0]
    li_r = row[0:1, :] + bi
    lf_r = _log_sigmoid(row[1:2, :] + bf)

    r_io = lax.broadcasted_iota(I32, (L, L), 0)
    c_io = lax.broadcasted_iota(I32, (L, L), 1)
    causal = r_io >= c_io
    tri = jnp.where(causal, 1.0, 0.0).astype(BF16)
    tri_t = jnp.where(r_io <= c_io, 1.0, 0.0).astype(BF16)
    b_c = sum(jnp.dot(tri, p, preferred_element_type=F32)
              for p in _split3(jnp.broadcast_to(lf_c, (L, LANES))))[:, 0:1]
    b_r = sum(jnp.dot(p, tri_t, preferred_element_type=F32)
              for p in _split3(jnp.broadcast_to(lf_r, (16, L))))[0:1, :]

    m_prev = ms_ref[0:1, 0:1]
    b_last = b_c[L - 1:L, :]
    dmat = jnp.where(causal, b_c - (b_r - li_r), -jnp.inf)
    inter = b_c + m_prev
    m_t = jnp.maximum(inter, jnp.max(dmat, axis=-1, keepdims=True))
    w_inter = jnp.exp(inter - m_t)
    s = lax.dot_general(q, k, (((1,), (1,)), ((), ())), preferred_element_type=F32)
    s = s * scale * jnp.exp(dmat - m_t)
    qc = jnp.dot(q, cb_ref[...], preferred_element_type=F32)
    num = w_inter * qc + jnp.dot(s.astype(BF16), v, preferred_element_type=F32)
    qn = jnp.sum(q.astype(F32) * ns_ref[...], axis=-1, keepdims=True)
    den = w_inter * qn + jnp.sum(s, axis=-1, keepdims=True)
    h = num / jnp.maximum(jnp.abs(den), jnp.exp(-m_t))
    hn = h * lax.rsqrt(jnp.mean(h * h, axis=-1, keepdims=True) + EPS)
    h_ref[...] = (hn * ng_ref[...] * og_ref[...]).astype(h_ref.dtype)

    g_c = b_last - b_c + li_c
    m_new = jnp.maximum(b_last + m_prev, jnp.max(g_c, axis=0, keepdims=True))
    w_s = jnp.exp(g_c - m_new)
    decay = jnp.exp(b_last + m_prev - m_new)
    kw = k.astype(F32) * (w_s * scale)
    kwb = kw.astype(BF16)
    cbw = min(_MLSTM_CB, dv)
    for j in range(dv // cbw):
        cs = slice(j * cbw, (j + 1) * cbw)
        kv = lax.dot_general(kwb, v[:, cs], (((0,), (0,)), ((), ())), preferred_element_type=F32)
        c_new = decay * c_ref[0, 0, :, cs] + kv
        c_ref[0, 0, :, cs] = c_new
        cb_ref[:, cs] = c_new.astype(BF16)
    n_new = decay * ns_ref[...] + jnp.sum(kw, axis=0, keepdims=True)
    ns_ref[...] = n_new
    ms_ref[...] = jnp.broadcast_to(m_new, ms_ref.shape)
    n_ref[0, 0] = n_new
    m_ref[0, 0] = jnp.broadcast_to(m_new, (1, LANES))


def _mlstm_prompt(qkv, og, if_pre, if_b, norm_g, batch, seq, n_heads, chunk):
    dh = qkv.shape[1] // (3 * n_heads)
    chunk = min(chunk, seq)
    nc = seq // chunk
    ifx = if_pre.reshape(batch, seq, 2, n_heads)
    ifc = ifx.transpose(0, 3, 1, 2)
    ifr = ifx.transpose(0, 3, 2, 1)
    row = lambda b, h, c: (b * nc + c)
    h_out, c_out, n_out, m_out = pl.pallas_call(
        functools.partial(_mlstm_prompt_kernel, n_heads=n_heads),
        grid=(batch, n_heads, nc),
        in_specs=[pl.BlockSpec(memory_space=pltpu.SMEM),
                  pl.BlockSpec((chunk, dh), lambda b, h, c: (row(b, h, c), h)),
                  pl.BlockSpec((chunk, dh), lambda b, h, c: (row(b, h, c), n_heads + h)),
                  pl.BlockSpec((chunk, dh), lambda b, h, c: (row(b, h, c), 2 * n_heads + h)),
                  pl.BlockSpec((1, 1, chunk, 2), lambda b, h, c: (b, h, c, 0)),
                  pl.BlockSpec((1, 1, 2, chunk), lambda b, h, c: (b, h, 0, c)),
                  pl.BlockSpec((chunk, dh), lambda b, h, c: (row(b, h, c), h)),
                  pl.BlockSpec((1, dh), lambda b, h, c: (0, h))],
        out_specs=[pl.BlockSpec((chunk, dh), lambda b, h, c: (row(b, h, c), h)),
                   pl.BlockSpec((1, 1, dh, dh), lambda b, h, c: (b, h, 0, 0)),
                   pl.BlockSpec((1, 1, 1, dh), lambda b, h, c: (b, h, 0, 0)),
                   pl.BlockSpec((1, 1, 1, LANES), lambda b, h, c: (b, h, 0, 0))],
        out_shape=[jax.ShapeDtypeStruct((batch * seq, n_heads * dh), BF16),
                   jax.ShapeDtypeStruct((batch, n_heads, dh, dh), F32),
                   jax.ShapeDtypeStruct((batch, n_heads, 1, dh), F32),
                   jax.ShapeDtypeStruct((batch, n_heads, 1, LANES), F32)],
        scratch_shapes=[pltpu.VMEM((dh, dh), BF16), pltpu.VMEM((1, dh), F32), pltpu.VMEM((8, LANES), F32)],
        compiler_params=_cparams("arbitrary", "arbitrary", "arbitrary"),
        name="mlstm_prompt",
    )(if_b, qkv, qkv, qkv, ifc, ifr, og, norm_g.reshape(1, -1))
    return h_out, c_out, n_out[:, :, 0, :], m_out[:, :, 0, 0]


def _mlstm_sample_kernel(bias_ref, ifp_ref, mst_ref, q_ref, k_ref, v_ref, c_ref, n_ref, og_ref, ng_ref,
                         h_ref, co_ref, no_ref, mo_ref, qt_ref, kt_ref, *, n_heads):
    hd = pl.program_id(0)
    b = pl.program_id(1)
    nb, dk = q_ref.shape
    dv = v_ref.shape[1]
    scale = dk ** -0.5

    @pl.when(b == 0)
    def _():
        for src, dst in ((q_ref, qt_ref), (k_ref, kt_ref)):
            for i, p in enumerate(_split3(src[...].T)):
                dst[i] = p

    li = jnp.full((1, 1), ifp_ref[b, hd] + bias_ref[hd], F32)
    lf = _log_sigmoid(jnp.full((1, 1), ifp_ref[b, n_heads + hd] + bias_ref[n_heads + hd], F32))
    m_prev = jnp.full((1, 1), mst_ref[b, hd], F32)
    m_t = jnp.maximum(lf + m_prev, li)
    decay = jnp.exp(lf + m_prev - m_t)
    w_s = jnp.exp(li - m_t)

    sel = jnp.where(lax.broadcasted_iota(I32, (nb, LANES), 0) == b, 1.0, 0.0).astype(BF16)
    qb = sum(jnp.dot(qt_ref[i], sel, preferred_element_type=F32) for i in range(3))
    kb = sum(jnp.dot(kt_ref[i], sel, preferred_element_type=F32) for i in range(3))

    q_row = q_ref[pl.ds(b, 1), :]
    k_row = k_ref[pl.ds(b, 1), :]
    v_row = v_ref[pl.ds(b, 1), :]
    n_row = n_ref[0, 0]
    sval = jnp.sum(q_row * k_row, axis=-1, keepdims=True) * scale * w_s
    den = decay * jnp.sum(q_row * n_row, axis=-1, keepdims=True) + sval
    denom = jnp.maximum(jnp.abs(den), jnp.exp(-m_t))
    wk = w_s * scale
    pieces = []
    for j in range(dv // LANES):
        ls = slice(j * LANES, (j + 1) * LANES)
        cj = c_ref[0, 0, :, ls]
        vj = v_row[:, ls]
        qc = jnp.sum(qb * cj, axis=0, keepdims=True)
        co_ref[0, 0, :, ls] = decay * cj + kb * (wk * vj)
        pieces.append((decay * qc + sval * vj) / denom)
    h = jnp.concatenate(pieces, axis=1)
    hn = h * lax.rsqrt(jnp.mean(h * h, axis=-1, keepdims=True) + EPS)
    h_ref[0] = (hn * ng_ref[...] * og_ref[0]).astype(h_ref.dtype)
    no_ref[0, 0] = decay * n_row + wk * k_row
    mo_ref[0, 0] = jnp.broadcast_to(m_t, (1, LANES))


def _mlstm_sample(qkv, og, if_pre, if_b, norm_g, st_c, st_n, st_m, n_heads):
    nb = qkv.shape[0]
    dh = qkv.shape[1] // (3 * n_heads)
    h_out, c_out, n_out, m_out = pl.pallas_call(
        functools.partial(_mlstm_sample_kernel, n_heads=n_heads),
        grid=(n_heads, nb),
        in_specs=[pl.BlockSpec(memory_space=pltpu.SMEM),
                  pl.BlockSpec(memory_space=pltpu.SMEM),
                  pl.BlockSpec(memory_space=pltpu.SMEM),
                  pl.BlockSpec((nb, dh), lambda h, b: (0, h)),
                  pl.BlockSpec((nb, dh), lambda h, b: (0, n_heads + h)),
                  pl.BlockSpec((nb, dh), lambda h, b: (0, 2 * n_heads + h)),
                  pl.BlockSpec((1, 1, dh, dh), lambda h, b: (b, h, 0, 0)),
                  pl.BlockSpec((1, 1, 1, dh), lambda h, b: (b, h, 0, 0)),
                  pl.BlockSpec((1, 1, dh), lambda h, b: (b, 0, h)),
                  pl.BlockSpec((1, dh), lambda h, b: (0, h))],
        out_specs=[pl.BlockSpec((1, 1, dh), lambda h, b: (b, 0, h)),
                   pl.BlockSpec((1, 1, dh, dh), lambda h, b: (b, h, 0, 0)),
                   pl.BlockSpec((1, 1, 1, dh), lambda h, b: (b, h, 0, 0)),
                   pl.BlockSpec((1, 1, 1, LANES), lambda h, b: (b, h, 0, 0))],
        out_shape=[jax.ShapeDtypeStruct((nb, 1, n_heads * dh), BF16),
                   jax.ShapeDtypeStruct((nb, n_heads, dh, dh), F32),
                   jax.ShapeDtypeStruct((nb, n_heads, 1, dh), F32),
                   jax.ShapeDtypeStruct((nb, n_heads, 1, LANES), F32)],
        scratch_shapes=[pltpu.VMEM((3, dh, nb), BF16), pltpu.VMEM((3, dh, nb), BF16)],
        compiler_params=_cparams("arbitrary", "arbitrary"),
        name="mlstm_sample",
    )(if_b, if_pre, st_m, qkv, qkv, qkv, st_c, st_n.reshape(nb, n_heads, 1, dh),
      og.reshape(nb, 1, n_heads * dh), norm_g.reshape(1, -1))
    return h_out.reshape(nb, n_heads * dh), c_out, n_out[:, :, 0, :], m_out[:, :, 0, 0]


def _topk_rows(s, n_take, payloads=()):
    iota = lax.broadcasted_iota(I32, s.shape, 0).astype(F32)
    big = float(s.shape[0])
    vals, poss, pays = [], [], [[] for _ in payloads]
    for _ in range(n_take):
        m = jnp.max(s, axis=0, keepdims=True)
        pos = jnp.min(jnp.where(s == m, iota, big), axis=0, keepdims=True)
        hit = iota == pos
        vals.append(m)
        poss.append(pos)
        for lst, p in zip(pays, payloads):
            lst.append(jnp.max(jnp.where(hit, p, -1.0), axis=0, keepdims=True))
        s = jnp.where(hit, -jnp.inf, s)
    return vals, poss, pays


def _peer_pairs(topk):
    return [(i, j) for i in range(topk) for j in range(topk) if (i + 1) * (j + 1) <= topk]


def _peer_topk_kernel(q_ref, sk_ref, a_ref, b_ref, g_ref, cv_ref, ci_ref, cj_ref, *, topk):
    tb = q_ref.shape[0]
    nk, dsub = sk_ref.shape[2], sk_ref.shape[3]
    pairs = _peer_pairs(topk)
    n_pad = cv_ref.shape[0]
    sk0 = sk_ref[0, 0].astype(BF16)
    sk1 = sk_ref[0, 1].astype(BF16)

    def group(gi, carry):
        t0 = pl.multiple_of(gi * LANES, LANES)
        sv, si = [], []
        for p, sk in ((0, sk0), (1, sk1)):
            qp = q_ref[pl.ds(t0, LANES), p * dsub:(p + 1) * dsub].astype(BF16)
            s = lax.dot_general(sk, qp, (((1,), (1,)), ((), ())), preferred_element_type=F32)
            vals, poss, _ = _topk_rows(s, topk)
            sv.append(vals)
            si.append(poss)
        cv_ref[...] = jnp.full(cv_ref.shape, -jnp.inf, F32)
        ci_ref[...] = jnp.zeros(ci_ref.shape, F32)
        cj_ref[...] = jnp.zeros(cj_ref.shape, F32)
        for r, (i, j) in enumerate(pairs):
            cv_ref[r:r + 1, :] = sv[0][i] + sv[1][j]
            ci_ref[r:r + 1, :] = si[0][i]
            cj_ref[r:r + 1, :] = si[1][j]
        tv, _, (ta, tbk) = _topk_rows(cv_ref[...], topk, (ci_ref[...], cj_ref[...]))
        ex = [jnp.exp(x - tv[0]) for x in tv]
        tot = sum(ex)
        for r in range(topk):
            a_ref[r:r + 1, pl.ds(t0, LANES)] = ta[r].astype(I32)
            b_ref[r:r + 1, pl.ds(t0, LANES)] = tbk[r].astype(I32)
            g_ref[r:r + 1, pl.ds(t0, LANES)] = ex[r] / tot
        return carry

    lax.fori_loop(0, tb // LANES, group, 0)


def _peer_topk(q, subkeys, topk, tb):
    t = q.shape[0]
    nh, _, nk, dsub = subkeys.shape
    tb = min(tb, t)
    n_pad = -(-len(_peer_pairs(topk)) // 8) * 8
    out = jax.ShapeDtypeStruct((nh * topk, t), I32)
    return pl.pallas_call(
        functools.partial(_peer_topk_kernel, topk=topk),
        grid=(t // tb, nh),
        in_specs=[pl.BlockSpec((tb, 2 * dsub), lambda i, h: (i, h)),
                  pl.BlockSpec((1, 2, nk, dsub), lambda i, h: (h, 0, 0, 0))],
        out_specs=[pl.BlockSpec((topk, tb), lambda i, h: (h, i))] * 3,
        out_shape=[out, out, jax.ShapeDtypeStruct((nh * topk, t), F32)],
        scratch_shapes=[pltpu.VMEM((n_pad, LANES), F32)] * 3,
        compiler_params=_cparams("arbitrary", "arbitrary"),
        name="peer_topk",
    )(q, subkeys)


def _peer_gates_kernel(a_ref, b_ref, g_ref, o_ref, at_ref, bt_ref, gt_ref):
    nk = o_ref.shape[1]
    tg = o_ref.shape[0]
    at_ref[...] = a_ref[...].T
    bt_ref[...] = b_ref[...].T
    gt_ref[...] = g_ref[...].T
    key = lax.broadcasted_iota(I32, (nk, at_ref.shape[1]), 0)

    def token(t, carry):
        a = at_ref[pl.ds(t, 1), :]
        b = bt_ref[pl.ds(t, 1), :]
        g = gt_ref[pl.ds(t, 1), :]
        first = jnp.where(key == a, g, 0.0).astype(BF16)
        second = jnp.where(key == b, 1.0, 0.0).astype(BF16)
        o_ref[t] = lax.dot_general(first, second, (((1,), (1,)), ((), ())), preferred_element_type=F32)
        return carry

    lax.fori_loop(0, tg, token, 0)


def _peer_gates(a_idx, b_idx, gates, nk, tg):
    hk, t = a_idx.shape
    tg = min(tg, t)
    return pl.pallas_call(
        _peer_gates_kernel,
        grid=(t // tg,),
        in_specs=[pl.BlockSpec((hk, tg), lambda i: (0, i))] * 3,
        out_specs=pl.BlockSpec((tg, nk, nk), lambda i: (i, 0, 0)),
        out_shape=jax.ShapeDtypeStruct((t, nk, nk), F32),
        scratch_shapes=[pltpu.VMEM((tg, hk), I32), pltpu.VMEM((tg, hk), I32), pltpu.VMEM((tg, hk), F32)],
        compiler_params=_cparams("arbitrary"),
        name="peer_gates",
    )(a_idx, b_idx, gates)


_SQRT_HALF = 0.7071067811865476


def _peer_dense_kernel(x_ref, u_ref, v_ref, g_ref, o_ref, *, nk):
    e = pl.program_id(1)

    @pl.when(e == 0)
    def _():
        o_ref[...] = jnp.zeros(o_ref.shape, F32)

    x = x_ref[...]
    n_first = g_ref.shape[1]
    for s in range(0, n_first, 2):
        rows = slice(s * nk, (s + 2) * nk)
        a = lax.dot_general(x, u_ref[rows, :], (((1,), (1,)), ((), ())), preferred_element_type=F32)
        act = 0.5 * a * (1.0 + lax.erf(a * _SQRT_HALF))
        g = jnp.concatenate([g_ref[:, s, :], g_ref[:, s + 1, :]], axis=1)
        coef = (g * act).astype(BF16)
        o_ref[...] += jnp.dot(coef, v_ref[rows, :], preferred_element_type=F32)


def _peer_dense(x, u_tab, v_tab, gmat, tb, n_first=8):
    t, d = x.shape
    nk = gmat.shape[1]
    tb = min(tb, t)
    eb = n_first * nk
    return pl.pallas_call(
        functools.partial(_peer_dense_kernel, nk=nk),
        grid=(t // tb, nk // n_first),
        in_specs=[pl.BlockSpec((tb, d), lambda i, e: (i, 0)),
                  pl.BlockSpec((eb, d), lambda i, e: (e, 0)),
                  pl.BlockSpec((eb, d), lambda i, e: (e, 0)),
                  pl.BlockSpec((tb, n_first, nk), lambda i, e: (i, e, 0))],
        out_specs=pl.BlockSpec((tb, d), lambda i, e: (i, 0)),
        out_shape=jax.ShapeDtypeStruct((t, d), F32),
        compiler_params=_cparams("arbitrary", "arbitrary"),
        name="peer_dense",
    )(x, u_tab, v_tab, gmat)


def _final_kernel(x_ref, p_ref, g_ref, fg_ref, o_ref):
    x = x_ref[...] + g_ref[0] * p_ref[...]
    o_ref[...] = x * lax.rsqrt(jnp.mean(x * x, axis=-1, keepdims=True) + EPS) * fg_ref[...]


def _final(x, p, mod3, g_idx, final_g, grp_rows, tm):
    t, d = x.shape
    r = mod3.shape[1]
    per = grp_rows // tm
    return pl.pallas_call(
        _final_kernel,
        grid=(t // tm,),
        in_specs=[pl.BlockSpec((tm, d), lambda i: (i, 0)),
                  pl.BlockSpec((tm, d), lambda i: (i, 0)),
                  pl.BlockSpec((1, r, d), lambda i: (i // per, 0, g_idx)),
                  pl.BlockSpec((1, d), lambda i: (0, 0))],
        out_specs=pl.BlockSpec((tm, d), lambda i: (i, 0)),
        out_shape=jax.ShapeDtypeStruct((t, d), F32),
        compiler_params=_cparams("arbitrary"),
        name="final_norm",
    )(x, p, mod3, final_g.reshape(1, d))


def _layer(x, mod3, grp_rows, state, w, n_heads, prompt_shape):
    t, d = x.shape
    (norm1_g, norm2_g, w_in, w_if, w_gate, if_b, conv_dw, conv_db, conv_ln_g, conv_ln_b, w_conv_out,
     mlstm_norm_g, w_mlstm_out, w_o, peer_wq, peer_subkeys, peer_u, peer_v) = w
    d_ml = mlstm_norm_g.shape[0]
    tm = min(512, grp_rows, t)
    h1 = _norm_mod(x, norm1_g, mod3, 0, 1, grp_rows, tm)
    u = _mm_glu(h1, w_in, d_half=d, tm=tm, tn=512)
    qkv_dtype = BF16 if prompt_shape is not None else F32
    qkv = _mm(h1, w_in, col0=2 * d, ncols=3 * d_ml, tm=tm, tn=1024, out_dtype=qkv_dtype, name="mm_qkv")
    og = _mm(h1, w_in, col0=2 * d + 3 * d_ml, ncols=d_ml, tm=tm, tn=1024, out_dtype=F32, act="sigmoid",
             name="mm_ogate")
    if_pre = _mm(h1, w_if, col0=0, ncols=LANES, tm=tm, tn=LANES, out_dtype=F32, name="mm_if")[:, :2 * n_heads]
    gates = _mm(h1, w_gate, col0=0, ncols=2 * d, tm=tm, tn=1024, out_dtype=F32, act="sigmoid", name="mm_gates")

    if prompt_shape is not None:
        bsz, seq = prompt_shape
        u3 = u.reshape(bsz, seq, d)
        y = _conv_prompt(u3, conv_dw, conv_db, conv_ln_g, conv_ln_b, tt=256).reshape(t, d)
        new_buf = u3[:, seq - (conv_dw.shape[0] - 1):, :]
        hm, c_new, n_new, m_new = _mlstm_prompt(qkv, og, if_pre, if_b, mlstm_norm_g, bsz, seq, n_heads, 256)
    else:
        st_conv, st_c, st_n, st_m = state
        y, new_buf = _conv_sample(st_conv, u.reshape(t, 1, d), conv_dw, conv_db, conv_ln_g, conv_ln_b)
        y = y.reshape(t, d)
        hm, c_new, n_new, m_new = _mlstm_sample(qkv, og, if_pre, if_b, mlstm_norm_g, st_c, st_n, st_m, n_heads)

    conv_out = _mm(y, w_conv_out, col0=0, ncols=d, tm=tm, tn=1024, out_dtype=F32, name="mm_conv_out")
    m_out = _mm(hm, w_mlstm_out, col0=0, ncols=d, tm=tm, tn=512, out_dtype=F32, name="mm_mlstm_out")
    x1 = _mm_merge(gates, conv_out, m_out, w_o, x, mod3, 2, grp_rows, min(tm, 256), 512)

    h2 = _norm_mod(x1, norm2_g, mod3, 3, 4, grp_rows, tm)
    pq = _mm(h2, peer_wq, col0=0, ncols=peer_wq.shape[1], tm=tm, tn=1024, out_dtype=F32, name="mm_peer_q")
    a_idx, b_idx, pg = _peer_topk(pq, peer_subkeys, PEER_TOPK, tb=512)
    gmat = _peer_gates(a_idx, b_idx, pg, peer_subkeys.shape[2], tg=LANES)
    p = _peer_dense(h2, peer_u, peer_v, gmat, tb=512)
    return x1, p, new_buf, c_new, n_new, m_new


def kernel(x_prompt, x_sample, state_conv, state_mlstm_c, state_mlstm_n, state_mlstm_m, c_prompt, c_sample,
           ada_w, ada_b, norm1_g, norm2_g, w_in, mlstm_if_b, conv_dw, conv_db, conv_ln_g, conv_ln_b,
           w_conv_out, mlstm_norm_g, w_mlstm_out, w_o, peer_wq, peer_subkeys, peer_u, peer_v, final_g):
    bsz, seq, d = x_prompt.shape
    nb, dseq, _ = x_sample.shape
    assert dseq == 1
    assert ada_w.shape[0] == 1, "single-layer stack only"
    n_heads = mlstm_if_b.shape[1] // 2
    d_ml = mlstm_norm_g.shape[1]
    col_if = 2 * d + 4 * d_ml
    xp = x_prompt.reshape(bsz * seq, d)
    xs = x_sample.reshape(nb, d)
    n_c = nb + bsz
    c_all = jnp.pad(jnp.concatenate([c_sample, c_prompt], axis=0), ((0, (-n_c) % 16), (0, 0)))
    mod = _ada(c_all, ada_w[0], ada_b[0])
    mod_s = mod[:nb].reshape(1, nb, 6 * d)
    mod_p = mod[nb:n_c].reshape(bsz, 1, 6 * d)
    w_if = jnp.pad(w_in[0][:, col_if:col_if + 2 * n_heads], ((0, 0), (0, LANES - 2 * n_heads)))
    w_gate = w_in[0][:, col_if + 2 * n_heads:]
    w = (norm1_g[0], norm2_g[0], w_in[0], w_if, w_gate, mlstm_if_b[0], conv_dw[0], conv_db[0], conv_ln_g[0],
         conv_ln_b[0], w_conv_out[0], mlstm_norm_g[0], w_mlstm_out[0], w_o[0], peer_wq[0], peer_subkeys[0],
         peer_u[0].astype(BF16), peer_v[0].astype(BF16))
    xp, pp, bp, cp, np_, mp = _layer(xp, mod_p, seq, None, w, n_heads, (bsz, seq))
    xs, ps, bs, cs, ns, ms = _layer(xs, mod_s, nb,
                                    (state_conv[0], state_mlstm_c[0], state_mlstm_n[0], state_mlstm_m[0]),
                                    w, n_heads, None)
    y_prompt = _final(xp, pp, mod_p, 5, final_g, seq, min(512, seq)).reshape(bsz, seq, d)
    y_sample = _final(xs, ps, mod_s, 5, final_g, nb, nb).reshape(nb, 1, d)
    return (y_prompt, y_sample) + tuple(o[None] for o in (bp, bs, cp, cs, np_, ns, mp, ms))
```

```python
import functools

import jax
import jax.numpy as jnp
from jax import lax
from jax.experimental import pallas as pl
from jax.experimental.pallas import tpu as pltpu

F32 = jnp.float32
BF16 = jnp.bfloat16
I32 = jnp.int32

EPS = 1e-6
NEG_INIT = -1e30
PEER_TOPK = 16
LANES = 128
VMEM_LIMIT = 56 * 1024 * 1024


def _cparams(*sem):
    return pltpu.CompilerParams(dimension_semantics=sem, vmem_limit_bytes=VMEM_LIMIT)


def _tile(target, *extents):
    t = min([target] + [e for e in extents if e])
    while any(e % t for e in extents):
        t -= LANES
    return t


def _sigmoid(x):
    return 1.0 / (1.0 + jnp.exp(-x))


def _log_sigmoid(x):
    return jnp.minimum(x, 0.0) - jnp.log1p(jnp.exp(-jnp.abs(x)))


def _split3(x):
    hi = x.astype(BF16)
    r1 = x - hi.astype(F32)
    mid = r1.astype(BF16)
    lo = (r1 - mid.astype(F32)).astype(BF16)
    return hi, mid, lo


def _ada_kernel(c_ref, w_ref, b_ref, o_ref):
    c = c_ref[...]
    a = (c * _sigmoid(c)).astype(BF16)
    o_ref[...] = jnp.dot(a, w_ref[...].astype(BF16), preferred_element_type=F32) + b_ref[...]


def _ada(c, w, b):
    m, d = c.shape
    n = w.shape[1]
    tn = _tile(1024, n)
    return pl.pallas_call(
        _ada_kernel,
        grid=(n // tn,),
        in_specs=[pl.BlockSpec((m, d), lambda j: (0, 0)),
                  pl.BlockSpec((d, tn), lambda j: (0, j)),
                  pl.BlockSpec((1, tn), lambda j: (0, j))],
        out_specs=pl.BlockSpec((m, tn), lambda j: (0, j)),
        out_shape=jax.ShapeDtypeStruct((m, n), F32),
        compiler_params=_cparams("arbitrary"),
        name="ada_mod",
    )(c, w, b.reshape(1, n))


def _norm_mod_kernel(x_ref, g_ref, sc_ref, sh_ref, o_ref):
    x = x_ref[...]
    y = x * lax.rsqrt(jnp.mean(x * x, axis=-1, keepdims=True) + EPS)
    o_ref[...] = (y * g_ref[...] * (1.0 + sc_ref[0]) + sh_ref[0]).astype(o_ref.dtype)


def _norm_mod(x, g, mod3, sh_idx, sc_idx, grp_rows, tm):
    t, d = x.shape
    r = mod3.shape[1]
    per = grp_rows // tm
    return pl.pallas_call(
        _norm_mod_kernel,
        grid=(t // tm,),
        in_specs=[pl.BlockSpec((tm, d), lambda i: (i, 0)),
                  pl.BlockSpec((1, d), lambda i: (0, 0)),
                  pl.BlockSpec((1, r, d), lambda i: (i // per, 0, sc_idx)),
                  pl.BlockSpec((1, r, d), lambda i: (i // per, 0, sh_idx))],
        out_specs=pl.BlockSpec((tm, d), lambda i: (i, 0)),
        out_shape=jax.ShapeDtypeStruct((t, d), BF16),
        compiler_params=_cparams("arbitrary"),
        name="norm_mod",
    )(x, g.reshape(1, d), mod3, mod3)


def _load_weight_tile(w_ref, transposed):
    w = w_ref[...]
    return (w.T if transposed else w).astype(BF16)


def _mm_kernel(*refs, act, has_mul, has_add, has_res, w_transposed):
    a_ref, w_ref = refs[0], refs[1]
    o_ref, wb_ref = refs[-2], refs[-1]
    extra = list(refs[2:-2])

    @pl.when(pl.program_id(1) == 0)
    def _():
        wb_ref[...] = _load_weight_tile(w_ref, w_transposed)

    acc = jnp.dot(a_ref[...], wb_ref[...], preferred_element_type=F32)
    if act == "sigmoid":
        acc = _sigmoid(acc)
    if has_mul:
        acc = acc * extra.pop(0)[...]
    if has_add:
        acc = acc + extra.pop(0)[...]
    if has_res:
        x_ref, g_ref = extra
        acc = x_ref[...] + g_ref[0] * acc
    o_ref[...] = acc.astype(o_ref.dtype)


def _mm(a, w, *, col0, ncols, tm, tn, out_dtype, act=None, mul=None, add=None, res=None, w_transposed=False,
        name="mm"):
    m, k = a.shape
    tm = min(tm, m)
    tn = _tile(tn, ncols, col0, mul[1] if mul else 0)
    cb = col0 // tn
    nj = ncols // tn
    w_spec = (pl.BlockSpec((tn, k), lambda j, i: (cb + j, 0)) if w_transposed
              else pl.BlockSpec((k, tn), lambda j, i: (0, cb + j)))
    in_specs = [pl.BlockSpec((tm, k), lambda j, i: (i, 0)), w_spec]
    operands = [a, w]
    tile = pl.BlockSpec((tm, tn), lambda j, i: (i, j))
    if mul is not None:
        mb = mul[1] // tn
        in_specs.append(pl.BlockSpec((tm, tn), lambda j, i: (i, mb + j)))
        operands.append(mul[0])
    if add is not None:
        in_specs.append(tile)
        operands.append(add)
    if res is not None:
        x, mod3, g_idx, grp_rows = res
        per = grp_rows // tm
        in_specs += [tile, pl.BlockSpec((1, mod3.shape[1], tn), lambda j, i: (i // per, 0, g_idx * nj + j))]
        operands += [x, mod3]
    return pl.pallas_call(
        functools.partial(_mm_kernel, act=act, has_mul=mul is not None, has_add=add is not None,
                          has_res=res is not None, w_transposed=w_transposed),
        grid=(nj, m // tm),
        in_specs=in_specs,
        out_specs=tile,
        out_shape=jax.ShapeDtypeStruct((m, ncols), out_dtype),
        scratch_shapes=[pltpu.VMEM((k, tn), BF16)],
        compiler_params=_cparams("arbitrary", "arbitrary"),
        name=name,
    )(*operands)


def _mm_glu_kernel(a_ref, wa_ref, wg_ref, o_ref, wab_ref, wgb_ref):
    @pl.when(pl.program_id(1) == 0)
    def _():
        wab_ref[...] = _load_weight_tile(wa_ref, True)
        wgb_ref[...] = _load_weight_tile(wg_ref, True)

    a = a_ref[...]
    lin = jnp.dot(a, wab_ref[...], preferred_element_type=F32)
    gate = jnp.dot(a, wgb_ref[...], preferred_element_type=F32)
    o_ref[...] = lin * _sigmoid(gate)


def _mm_glu(a, w_t, *, d_half, tm, tn):
    m, k = a.shape
    tm = min(tm, m)
    tn = _tile(tn, d_half)
    off = d_half // tn
    return pl.pallas_call(
        _mm_glu_kernel,
        grid=(d_half // tn, m // tm),
        in_specs=[pl.BlockSpec((tm, k), lambda j, i: (i, 0)),
                  pl.BlockSpec((tn, k), lambda j, i: (j, 0)),
                  pl.BlockSpec((tn, k), lambda j, i: (off + j, 0))],
        out_specs=pl.BlockSpec((tm, tn), lambda j, i: (i, j)),
        out_shape=jax.ShapeDtypeStruct((m, d_half), F32),
        scratch_shapes=[pltpu.VMEM((k, tn), BF16), pltpu.VMEM((k, tn), BF16)],
        compiler_params=_cparams("arbitrary", "arbitrary"),
        name="mm_glu",
    )(a, w_t, w_t)


_HALO = 32
_CONV_RC = 64


def _ln_silu(y, lg, lb):
    mu = jnp.mean(y, axis=-1, keepdims=True)
    yc = y - mu
    var = jnp.mean(yc * yc, axis=-1, keepdims=True)
    z = yc * lax.rsqrt(var + EPS) * lg + lb
    return z * _sigmoid(z)


def _conv_prompt_kernel(u_ref, dw_ref, db_ref, lg_ref, lb_ref, o_ref, win_ref, y_ref, sh_ref, *, tt, kw):
    d = u_ref.shape[2]
    t = pl.program_id(1)

    @pl.when(t == 0)
    def _():
        win_ref[0:_HALO, :] = jnp.zeros((_HALO, d), F32)

    @pl.when(t > 0)
    def _():
        win_ref[0:_HALO, :] = win_ref[tt:tt + _HALO, :]

    win_ref[_HALO:_HALO + tt, :] = u_ref[0]
    base = _HALO - (kw - 1)
    rc = min(_CONV_RC, tt)
    for mis in range(1, 8):
        sh_ref[mis - 1] = win_ref[mis:mis + tt + _HALO - 8, :]

    def row_chunk(r, carry):
        r0 = pl.multiple_of(r * rc, rc)
        for c in range(d // LANES):
            ls = slice(c * LANES, (c + 1) * LANES)
            acc = jnp.broadcast_to(db_ref[:, ls], (rc, LANES))
            for mis in range(8):
                taps = [j for j in range(kw) if (base + j) % 8 == mis]
                span = max(base + j - mis for j in taps) + rc
                w = win_ref[pl.ds(r0, span), ls] if mis == 0 else sh_ref[mis - 1, pl.ds(r0, span), ls]
                for j in taps:
                    off = base + j - mis
                    acc = acc + w[off:off + rc, :] * dw_ref[j:j + 1, ls]
            y_ref[pl.ds(r0, rc), ls] = acc
        return carry

    lax.fori_loop(0, tt // rc, row_chunk, 0)
    o_ref[0] = _ln_silu(y_ref[...], lg_ref[...], lb_ref[...]).astype(o_ref.dtype)


def _conv_prompt(u3, dw, db, lg, lb, tt):
    b, t, d = u3.shape
    tt = min(tt, t)
    kw = dw.shape[0]
    assert kw - 1 <= _HALO <= tt
    return pl.pallas_call(
        functools.partial(_conv_prompt_kernel, tt=tt, kw=kw),
        grid=(b, t // tt),
        in_specs=[pl.BlockSpec((1, tt, d), lambda i, j: (i, j, 0)),
                  pl.BlockSpec((kw, d), lambda i, j: (0, 0)),
                  pl.BlockSpec((1, d), lambda i, j: (0, 0)),
                  pl.BlockSpec((1, d), lambda i, j: (0, 0)),
                  pl.BlockSpec((1, d), lambda i, j: (0, 0))],
        out_specs=pl.BlockSpec((1, tt, d), lambda i, j: (i, j, 0)),
        out_shape=jax.ShapeDtypeStruct((b, t, d), BF16),
        scratch_shapes=[pltpu.VMEM((_HALO + tt, d), F32), pltpu.VMEM((tt, d), F32),
                        pltpu.VMEM((7, tt + _HALO - 8, d), F32)],
        compiler_params=_cparams("arbitrary", "arbitrary"),
        name="conv_prompt",
    )(u3, dw, db.reshape(1, d), lg.reshape(1, d), lb.reshape(1, d))


def _conv_sample_kernel(buf_ref, u_ref, dw_ref, db_ref, lg_ref, lb_ref, o_ref, nb_ref, *, kw):
    u = u_ref[...]
    acc = u * dw_ref[kw - 1:kw, :] + db_ref[...]
    for j in range(kw - 1):
        acc = acc + buf_ref[j] * dw_ref[j:j + 1, :]
    o_ref[...] = _ln_silu(acc, lg_ref[...], lb_ref[...]).astype(o_ref.dtype)
    for j in range(kw - 2):
        nb_ref[j] = buf_ref[j + 1]
    nb_ref[kw - 2] = u


def _conv_sample(buf_t, u, dw, db, lg, lb, bb=16):
    b, d = u.shape
    kw = dw.shape[0]
    bb = min(bb, b)
    return pl.pallas_call(
        functools.partial(_conv_sample_kernel, kw=kw),
        grid=(b // bb,),
        in_specs=[pl.BlockSpec((kw - 1, bb, d), lambda i: (0, i, 0)),
                  pl.BlockSpec((bb, d), lambda i: (i, 0)),
                  pl.BlockSpec((kw, d), lambda i: (0, 0)),
                  pl.BlockSpec((1, d), lambda i: (0, 0)),
                  pl.BlockSpec((1, d), lambda i: (0, 0)),
                  pl.BlockSpec((1, d), lambda i: (0, 0))],
        out_specs=[pl.BlockSpec((bb, d), lambda i: (i, 0)),
                   pl.BlockSpec((kw - 1, bb, d), lambda i: (0, i, 0))],
        out_shape=[jax.ShapeDtypeStruct((b, d), BF16),
                   jax.ShapeDtypeStruct((kw - 1, b, d), F32)],
        compiler_params=_cparams("arbitrary"),
        name="conv_sample",
    )(buf_t, u, dw, db.reshape(1, d), lg.reshape(1, d), lb.reshape(1, d))


_MLSTM_CB = 256


def _mlstm_prompt_kernel(bias_ref, q_ref, k_ref, v_ref, ifc_ref, ifr_ref, og_ref, ng_ref,
                         h_ref, c_ref, n_ref, m_ref, cb_ref, ns_ref, ms_ref, *, n_heads):
    hd = pl.program_id(1)
    ck = pl.program_id(2)
    L, dk = q_ref.shape
    dv = v_ref.shape[1]
    scale = dk ** -0.5

    @pl.when(ck == 0)
    def _():
        c_ref[...] = jnp.zeros(c_ref.shape, F32)
        cb_ref[...] = jnp.zeros(cb_ref.shape, BF16)
        ns_ref[...] = jnp.zeros(ns_ref.shape, F32)
        ms_ref[...] = jnp.full(ms_ref.shape, NEG_INIT, F32)

    q = q_ref[...]
    k = k_ref[...]
    v = v_ref[...]
    bi = bias_ref[hd]
    bf = bias_ref[n_heads + hd]
    col = ifc_ref[0, 0]
    li_c = col[:, 0:1] + bi
    lf_c = _log_sigmoid(col[:, 1:2] + bf)
    row = ifr_ref[0, 0]
    li_r = row[0:1, :] + bi
    lf_r = _log_sigmoid(row[1:2, :] + bf)

    r_io = lax.broadcasted_iota(I32, (L, L), 0)
    c_io = lax.broadcasted_iota(I32, (L, L), 1)
    causal = r_io >= c_io
    tri = jnp.where(causal, 1.0, 0.0).astype(BF16)
    tri_t = jnp.where(r_io <= c_io, 1.0, 0.0).astype(BF16)
    b_c = sum(jnp.dot(tri, p, preferred_element_type=F32)
              for p in _split3(jnp.broadcast_to(lf_c, (L, LANES))))[:, 0:1]
    b_r = sum(jnp.dot(p, tri_t, preferred_element_type=F32)
              for p in _split3(jnp.broadcast_to(lf_r, (16, L))))[0:1, :]

    m_prev = ms_ref[0:1, 0:1]
    b_last = b_c[L - 1:L, :]
    dmat = jnp.where(causal, b_c - (b_r - li_r), -jnp.inf)
    inter = b_c + m_prev
    m_t = jnp.maximum(inter, jnp.max(dmat, axis=-1, keepdims=True))
    w_inter = jnp.exp(inter - m_t)
    s = lax.dot_general(q, k, (((1,), (1,)), ((), ())), preferred_element_type=F32)
    s = s * scale * jnp.exp(dmat - m_t)
    qc = jnp.dot(q, cb_ref[...], preferred_element_type=F32)
    num = w_inter * qc + jnp.dot(s.astype(BF16), v, preferred_element_type=F32)
    qn = jnp.sum(q.astype(F32) * ns_ref[...], axis=-1, keepdims=True)
    den = w_inter * qn + jnp.sum(s, axis=-1, keepdims=True)
    h = num / jnp.maximum(jnp.abs(den), jnp.exp(-m_t))
    hn = h * lax.rsqrt(jnp.mean(h * h, axis=-1, keepdims=True) + EPS)
    h_ref[...] = (hn * ng_ref[...] * og_ref[...]).astype(h_ref.dtype)

    g_c = b_last - b_c + li_c
    m_new = jnp.maximum(b_last + m_prev, jnp.max(g_c, axis=0, keepdims=True))
    w_s = jnp.exp(g_c - m_new)
    decay = jnp.exp(b_last + m_prev - m_new)
    kw = k.astype(F32) * (w_s * scale)
    kwb = kw.astype(BF16)
    cbw = min(_MLSTM_CB, dv)
    for j in range(dv // cbw):
        cs = slice(j * cbw, (j + 1) * cbw)
        kv = lax.dot_general(kwb, v[:, cs], (((0,), (0,)), ((), ())), preferred_element_type=F32)
        c_new = decay * c_ref[0, 0, :, cs] + kv
        c_ref[0, 0, :, cs] = c_new
        cb_ref[:, cs] = c_new.astype(BF16)
    n_new = decay * ns_ref[...] + jnp.sum(kw, axis=0, keepdims=True)
    ns_ref[...] = n_new
    ms_ref[...] = jnp.broadcast_to(m_new, ms_ref.shape)
    n_ref[0, 0] = n_new
    m_ref[0, 0] = jnp.broadcast_to(m_new, (1, LANES))


def _mlstm_prompt(qkv, og, if_pre, if_b, norm_g, batch, seq, n_heads, chunk):
    dh = qkv.shape[1] // (3 * n_heads)
    chunk = min(chunk, seq)
    nc = seq // chunk
    ifx = if_pre.reshape(batch, seq, 2, n_heads)
    ifc = ifx.transpose(0, 3, 1, 2)
    ifr = ifx.transpose(0, 3, 2, 1)
    row = lambda b, h, c: (b * nc + c)
    h_out, c_out, n_out, m_out = pl.pallas_call(
        functools.partial(_mlstm_prompt_kernel, n_heads=n_heads),
        grid=(batch, n_heads, nc),
        in_specs=[pl.BlockSpec(memory_space=pltpu.SMEM),
                  pl.BlockSpec((chunk, dh), lambda b, h, c: (row(b, h, c), h)),
                  pl.BlockSpec((chunk, dh), lambda b, h, c: (row(b, h, c), n_heads + h)),
                  pl.BlockSpec((chunk, dh), lambda b, h, c: (row(b, h, c), 2 * n_heads + h)),
                  pl.BlockSpec((1, 1, chunk, 2), lambda b, h, c: (b, h, c, 0)),
                  pl.BlockSpec((1, 1, 2, chunk), lambda b, h, c: (b, h, 0, c)),
                  pl.BlockSpec((chunk, dh), lambda b, h, c: (row(b, h, c), h)),
                  pl.BlockSpec((1, dh), lambda b, h, c: (0, h))],
        out_specs=[pl.BlockSpec((chunk, dh), lambda b, h, c: (row(b, h, c), h)),
                   pl.BlockSpec((1, 1, dh, dh), lambda b, h, c: (b, h, 0, 0)),
                   pl.BlockSpec((1, 1, 1, dh), lambda b, h, c: (b, h, 0, 0)),
                   pl.BlockSpec((1, 1, 1, LANES), lambda b, h, c: (b, h, 0, 0))],
        out_shape=[jax.ShapeDtypeStruct((batch * seq, n_heads * dh), BF16),
                   jax.ShapeDtypeStruct((batch, n_heads, dh, dh), F32),
                   jax.ShapeDtypeStruct((batch, n_heads, 1, dh), F32),
                   jax.ShapeDtypeStruct((batch, n_heads, 1, LANES), F32)],
        scratch_shapes=[pltpu.VMEM((dh, dh), BF16), pltpu.VMEM((1, dh), F32), pltpu.VMEM((8, LANES), F32)],
        compiler_params=_cparams("arbitrary", "arbitrary", "arbitrary"),
        name="mlstm_prompt",
    )(if_b, qkv, qkv, qkv, ifc, ifr, og, norm_g.reshape(1, -1))
    return h_out, c_out, n_out[:, :, 0, :], m_out[:, :, 0, 0]


def _mlstm_sample_kernel(bias_ref, ifp_ref, mst_ref, q_ref, k_ref, v_ref, c_ref, n_ref, og_ref, ng_ref,
                         h_ref, co_ref, no_ref, mo_ref, qt_ref, kt_ref, *, n_heads):
    hd = pl.program_id(0)
    b = pl.program_id(1)
    nb, dk = q_ref.shape
    dv = v_ref.shape[1]
    scale = dk ** -0.5

    @pl.when(b == 0)
    def _():
        for src, dst in ((q_ref, qt_ref), (k_ref, kt_ref)):
            for i, p in enumerate(_split3(src[...].T)):
                dst[i] = p

    li = jnp.full((1, 1), ifp_ref[b, hd] + bias_ref[hd], F32)
    lf = _log_sigmoid(jnp.full((1, 1), ifp_ref[b, n_heads + hd] + bias_ref[n_heads + hd], F32))
    m_prev = jnp.full((1, 1), mst_ref[b, hd], F32)
    m_t = jnp.maximum(lf + m_prev, li)
    decay = jnp.exp(lf + m_prev - m_t)
    w_s = jnp.exp(li - m_t)

    sel = jnp.where(lax.broadcasted_iota(I32, (nb, LANES), 0) == b, 1.0, 0.0).astype(BF16)
    qb = sum(jnp.dot(qt_ref[i], sel, preferred_element_type=F32) for i in range(3))
    kb = sum(jnp.dot(kt_ref[i], sel, preferred_element_type=F32) for i in range(3))

    q_row = q_ref[pl.ds(b, 1), :]
    k_row = k_ref[pl.ds(b, 1), :]
    v_row = v_ref[pl.ds(b, 1), :]
    n_row = n_ref[0, 0]
    sval = jnp.sum(q_row * k_row, axis=-1, keepdims=True) * scale * w_s
    den = decay * jnp.sum(q_row * n_row, axis=-1, keepdims=True) + sval
    denom = jnp.maximum(jnp.abs(den), jnp.exp(-m_t))
    wk = w_s * scale
    pieces = []
    for j in range(dv // LANES):
        ls = slice(j * LANES, (j + 1) * LANES)
        cj = c_ref[0, 0, :, ls]
        vj = v_row[:, ls]
        qc = jnp.sum(qb * cj, axis=0, keepdims=True)
        co_ref[0, 0, :, ls] = decay * cj + kb * (wk * vj)
        pieces.append((decay * qc + sval * vj) / denom)
    h = jnp.concatenate(pieces, axis=1)
    hn = h * lax.rsqrt(jnp.mean(h * h, axis=-1, keepdims=True) + EPS)
    h_ref[0] = (hn * ng_ref[...] * og_ref[0]).astype(h_ref.dtype)
    no_ref[0, 0] = decay * n_row + wk * k_row
    mo_ref[0, 0] = jnp.broadcast_to(m_t, (1, LANES))


def _mlstm_sample(qkv, og, if_pre, if_b, norm_g, st_c, st_n, st_m, n_heads):
    nb = qkv.shape[0]
    dh = qkv.shape[1] // (3 * n_heads)
    h_out, c_out, n_out, m_out = pl.pallas_call(
        functools.partial(_mlstm_sample_kernel, n_heads=n_heads),
        grid=(n_heads, nb),
        in_specs=[pl.BlockSpec(memory_space=pltpu.SMEM),
                  pl.BlockSpec(memory_space=pltpu.SMEM),
                  pl.BlockSpec(memory_space=pltpu.SMEM),
                  pl.BlockSpec((nb, dh), lambda h, b: (0, h)),
                  pl.BlockSpec((nb, dh), lambda h, b: (0, n_heads + h)),
                  pl.BlockSpec((nb, dh), lambda h, b: (0, 2 * n_heads + h)),
                  pl.BlockSpec((1, 1, dh, dh), lambda h, b: (b, h, 0, 0)),
                  pl.BlockSpec((1, 1, 1, dh), lambda h, b: (b, h, 0, 0)),
                  pl.BlockSpec((1, 1, dh), lambda h, b: (b, 0, h)),
                  pl.BlockSpec((1, dh), lambda h, b: (0, h))],
        out_specs=[pl.BlockSpec((1, 1, dh), lambda h, b: (b, 0, h)),
                   pl.BlockSpec((1, 1, dh, dh), lambda h, b: (b, h, 0, 0)),
                   pl.BlockSpec((1, 1, 1, dh), lambda h, b: (b, h, 0, 0)),
                   pl.BlockSpec((1, 1, 1, LANES), lambda h, b: (b, h, 0, 0))],
        out_shape=[jax.ShapeDtypeStruct((nb, 1, n_heads * dh), BF16),
                   jax.ShapeDtypeStruct((nb, n_heads, dh, dh), F32),
                   jax.ShapeDtypeStruct((nb, n_heads, 1, dh), F32),
                   jax.ShapeDtypeStruct((nb, n_heads, 1, LANES), F32)],
        scratch_shapes=[pltpu.VMEM((3, dh, nb), BF16), pltpu.VMEM((3, dh, nb), BF16)],
        compiler_params=_cparams("arbitrary", "arbitrary"),
        name="mlstm_sample",
    )(if_b, if_pre, st_m, qkv, qkv, qkv, st_c, st_n.reshape(nb, n_heads, 1, dh),
      og.reshape(nb, 1, n_heads * dh), norm_g.reshape(1, -1))
    return h_out.reshape(nb, n_heads * dh), c_out, n_out[:, :, 0, :], m_out[:, :, 0, 0]


def _topk_rows(problems, n_take):
    state = []
    for s, payload in problems:
        iota = lax.broadcasted_iota(I32, s.shape, 0).astype(F32)
        state.append([s, iota, payload, [], []])
    for _ in range(n_take):
        for st in state:
            s, iota, payload, vals, picks = st
            m = jnp.max(s, axis=0, keepdims=True)
            pos = jnp.min(jnp.where(s == m, iota, float(s.shape[0])), axis=0, keepdims=True)
            hit = iota == pos
            vals.append(m)
            picks.append(pos if payload is None
                         else jnp.max(jnp.where(hit, payload, -1.0), axis=0, keepdims=True))
            st[0] = jnp.where(hit, -jnp.inf, s)
    return [(st[3], st[4]) for st in state]


def _peer_pairs(topk):
    return [(i, j) for i in range(topk) for j in range(topk) if (i + 1) * (j + 1) <= topk]


def _peer_topk_kernel(q_ref, sk_ref, a_ref, b_ref, g_ref, cv_ref, ce_ref, *, topk):
    tb = q_ref.shape[0]
    nk, dsub = sk_ref.shape[2], sk_ref.shape[3]
    pairs = _peer_pairs(topk)
    sk = [sk_ref[0, p].astype(BF16) for p in range(2)]

    n_groups = tb // LANES
    per_iter = cv_ref.shape[0]

    def groups(gi, carry):
        starts = [pl.multiple_of((gi * per_iter + n) * LANES, LANES) for n in range(per_iter)]
        for n, t0 in enumerate(starts):
            scores = []
            for p in range(2):
                qp = q_ref[pl.ds(t0, LANES), p * dsub:(p + 1) * dsub].astype(BF16)
                scores.append(lax.dot_general(sk[p], qp, (((1,), (1,)), ((), ())),
                                              preferred_element_type=F32))
            (sv0, si0), (sv1, si1) = _topk_rows([(scores[0], None), (scores[1], None)], topk)
            cv_ref[n] = jnp.full(cv_ref.shape[1:], -jnp.inf, F32)
            ce_ref[n] = jnp.zeros(ce_ref.shape[1:], F32)
            for r, (i, j) in enumerate(pairs):
                cv_ref[n, r:r + 1, :] = sv0[i] + sv1[j]
                ce_ref[n, r:r + 1, :] = si0[i] * float(nk) + si1[j]
        picked = _topk_rows([(cv_ref[n], ce_ref[n]) for n in range(per_iter)], topk)
        for t0, (tv, te) in zip(starts, picked):
            ex = [jnp.exp(x - tv[0]) for x in tv]
            tot = sum(ex)
            for r in range(topk):
                first = jnp.floor(te[r] * (1.0 / nk))
                a_ref[r:r + 1, pl.ds(t0, LANES)] = first.astype(I32)
                b_ref[r:r + 1, pl.ds(t0, LANES)] = (te[r] - first * float(nk)).astype(I32)
                g_ref[r:r + 1, pl.ds(t0, LANES)] = ex[r] / tot
        return carry

    lax.fori_loop(0, n_groups // per_iter, groups, 0)


def _peer_topk(q, subkeys, topk, tb):
    t = q.shape[0]
    nh, _, nk, dsub = subkeys.shape
    tb = min(tb, t)
    assert nk & (nk - 1) == 0 and nk * nk < 2 ** 24
    n_pad = -(-len(_peer_pairs(topk)) // 8) * 8
    per_iter = 2 if (tb // LANES) % 2 == 0 else 1
    out = jax.ShapeDtypeStruct((nh * topk, t), I32)
    return pl.pallas_call(
        functools.partial(_peer_topk_kernel, topk=topk),
        grid=(t // tb, nh),
        in_specs=[pl.BlockSpec((tb, 2 * dsub), lambda i, h: (i, h)),
                  pl.BlockSpec((1, 2, nk, dsub), lambda i, h: (h, 0, 0, 0))],
        out_specs=[pl.BlockSpec((topk, tb), lambda i, h: (h, i))] * 3,
        out_shape=[out, out, jax.ShapeDtypeStruct((nh * topk, t), F32)],
        scratch_shapes=[pltpu.VMEM((per_iter, n_pad, LANES), F32)] * 2,
        compiler_params=_cparams("arbitrary", "arbitrary"),
        name="peer_topk",
    )(q, subkeys)


def _peer_gates_kernel(a_ref, b_ref, g_ref, o_ref, at_ref, bt_ref, gt_ref):
    nk = o_ref.shape[1]
    tg = o_ref.shape[0]
    at_ref[...] = a_ref[...].T
    bt_ref[...] = b_ref[...].T
    gt_ref[...] = g_ref[...].T
    key = lax.broadcasted_iota(I32, (nk, at_ref.shape[1]), 0)

    def token(t, carry):
        a = at_ref[pl.ds(t, 1), :]
        b = bt_ref[pl.ds(t, 1), :]
        g = gt_ref[pl.ds(t, 1), :]
        first = jnp.where(key == a, g, 0.0).astype(BF16)
        second = jnp.where(key == b, 1.0, 0.0).astype(BF16)
        o_ref[t] = lax.dot_general(first, second, (((1,), (1,)), ((), ())), preferred_element_type=F32)
        return carry

    lax.fori_loop(0, tg, token, 0, unroll=8)


def _peer_gates(a_idx, b_idx, gates, nk, tg):
    hk, t = a_idx.shape
    tg = min(tg, t)
    return pl.pallas_call(
        _peer_gates_kernel,
        grid=(t // tg,),
        in_specs=[pl.BlockSpec((hk, tg), lambda i: (0, i))] * 3,
        out_specs=pl.BlockSpec((tg, nk, nk), lambda i: (i, 0, 0)),
        out_shape=jax.ShapeDtypeStruct((t, nk, nk), F32),
        scratch_shapes=[pltpu.VMEM((tg, hk), I32), pltpu.VMEM((tg, hk), I32), pltpu.VMEM((tg, hk), F32)],
        compiler_params=_cparams("arbitrary"),
        name="peer_gates",
    )(a_idx, b_idx, gates)


_SQRT_HALF = 0.7071067811865476


def _peer_dense_kernel(x_ref, u_ref, v_ref, g_ref, o_ref, coef_ref):
    e = pl.program_id(1)
    cur = e % 2
    prev = 1 - cur

    @pl.when(e == 0)
    def _():
        o_ref[...] = jnp.zeros(o_ref.shape, F32)
        coef_ref[1] = jnp.zeros(coef_ref.shape[1:], BF16)

    a = lax.dot_general(x_ref[...], u_ref[...], (((1,), (1,)), ((), ())), preferred_element_type=F32)
    o_ref[...] += jnp.dot(coef_ref[prev], v_ref[...], preferred_element_type=F32)
    act = 0.5 * a * (1.0 + lax.erf(a * _SQRT_HALF))
    g = jnp.concatenate([g_ref[:, s, :] for s in range(g_ref.shape[1])], axis=1)
    coef_ref[cur] = (g * act).astype(BF16)


def _peer_dense(x, u_tab, v_tab, gmat, tb, n_first=8):
    t, d = x.shape
    nk = gmat.shape[1]
    tb = min(tb, t)
    eb = n_first * nk
    ne = nk // n_first
    cur_blk = lambda e: jnp.minimum(e, ne - 1)
    return pl.pallas_call(
        _peer_dense_kernel,
        grid=(t // tb, ne + 1),
        in_specs=[pl.BlockSpec((tb, d), lambda i, e: (i, 0)),
                  pl.BlockSpec((eb, d), lambda i, e: (cur_blk(e), 0)),
                  pl.BlockSpec((eb, d), lambda i, e: (jnp.maximum(e - 1, 0), 0)),
                  pl.BlockSpec((tb, n_first, nk), lambda i, e: (i, cur_blk(e), 0))],
        out_specs=pl.BlockSpec((tb, d), lambda i, e: (i, 0)),
        out_shape=jax.ShapeDtypeStruct((t, d), F32),
        scratch_shapes=[pltpu.VMEM((2, tb, eb), BF16)],
        compiler_params=_cparams("arbitrary", "arbitrary"),
        name="peer_dense",
    )(x, u_tab, v_tab, gmat)


def _final_kernel(x_ref, p_ref, g_ref, fg_ref, o_ref):
    x = x_ref[...] + g_ref[0] * p_ref[...]
    o_ref[...] = x * lax.rsqrt(jnp.mean(x * x, axis=-1, keepdims=True) + EPS) * fg_ref[...]


def _final(x, p, mod3, g_idx, final_g, grp_rows, tm):
    t, d = x.shape
    r = mod3.shape[1]
    per = grp_rows // tm
    return pl.pallas_call(
        _final_kernel,
        grid=(t // tm,),
        in_specs=[pl.BlockSpec((tm, d), lambda i: (i, 0)),
                  pl.BlockSpec((tm, d), lambda i: (i, 0)),
                  pl.BlockSpec((1, r, d), lambda i: (i // per, 0, g_idx)),
                  pl.BlockSpec((1, d), lambda i: (0, 0))],
        out_specs=pl.BlockSpec((tm, d), lambda i: (i, 0)),
        out_shape=jax.ShapeDtypeStruct((t, d), F32),
        compiler_params=_cparams("arbitrary"),
        name="final_norm",
    )(x, p, mod3, final_g.reshape(1, d))


def _layer(x, mod3, grp_rows, state, w, n_heads, prompt_shape):
    t, d = x.shape
    (norm1_g, norm2_g, w_in_t, w_if_t, w_gate_t, if_b, conv_dw, conv_db, conv_ln_g, conv_ln_b, w_conv_out,
     mlstm_norm_g, w_mlstm_out, w_o, peer_wq, peer_subkeys, peer_u, peer_v) = w
    d_ml = mlstm_norm_g.shape[0]
    tm = min(512, grp_rows, t)
    h1 = _norm_mod(x, norm1_g, mod3, 0, 1, grp_rows, tm)
    u = _mm_glu(h1, w_in_t, d_half=d, tm=tm, tn=512)
    qkv_dtype = BF16 if prompt_shape is not None else F32
    in_proj = functools.partial(_mm, h1, tm=tm, w_transposed=True)
    qkv = in_proj(w_in_t, col0=2 * d, ncols=3 * d_ml, tn=1024, out_dtype=qkv_dtype, name="mm_qkv")
    og = in_proj(w_in_t, col0=2 * d + 3 * d_ml, ncols=d_ml, tn=1024, out_dtype=F32, act="sigmoid", name="mm_ogate")
    if_pre = in_proj(w_if_t, col0=0, ncols=LANES, tn=LANES, out_dtype=F32, name="mm_if")[:, :2 * n_heads]
    gates = in_proj(w_gate_t, col0=0, ncols=2 * d, tn=1024, out_dtype=F32, act="sigmoid", name="mm_gates")

    if prompt_shape is not None:
        bsz, seq = prompt_shape
        u3 = u.reshape(bsz, seq, d)
        y = _conv_prompt(u3, conv_dw, conv_db, conv_ln_g, conv_ln_b, tt=256).reshape(t, d)
        new_buf = u3[:, seq - (conv_dw.shape[0] - 1):, :]
        hm, c_new, n_new, m_new = _mlstm_prompt(qkv, og, if_pre, if_b, mlstm_norm_g, bsz, seq, n_heads, 256)
    else:
        st_conv_t, st_c, st_n, st_m = state
        y, new_buf_t = _conv_sample(st_conv_t, u, conv_dw, conv_db, conv_ln_g, conv_ln_b)
        new_buf = new_buf_t.transpose(1, 0, 2)
        hm, c_new, n_new, m_new = _mlstm_sample(qkv, og, if_pre, if_b, mlstm_norm_g, st_c, st_n, st_m, n_heads)

    conv_g = _mm(y, w_conv_out, col0=0, ncols=d, tm=tm, tn=1024, out_dtype=F32, mul=(gates, 0), name="mm_conv_out")
    merged = _mm(hm, w_mlstm_out, col0=0, ncols=d, tm=tm, tn=512, out_dtype=BF16, mul=(gates, d), add=conv_g,
                 name="mm_mlstm_out")
    x1 = _mm(merged, w_o, col0=0, ncols=d, tm=tm, tn=1024, out_dtype=F32, res=(x, mod3, 2, grp_rows), name="mm_merge")

    h2 = _norm_mod(x1, norm2_g, mod3, 3, 4, grp_rows, tm)
    pq = _mm(h2, peer_wq, col0=0, ncols=peer_wq.shape[1], tm=tm, tn=1024, out_dtype=F32, name="mm_peer_q")
    a_idx, b_idx, pg = _peer_topk(pq, peer_subkeys, PEER_TOPK, tb=512)
    gmat = _peer_gates(a_idx, b_idx, pg, peer_subkeys.shape[2], tg=LANES)
    p = _peer_dense(h2, peer_u, peer_v, gmat, tb=512)
    return x1, p, new_buf, c_new, n_new, m_new


def kernel(x_prompt, x_sample, state_conv, state_mlstm_c, state_mlstm_n, state_mlstm_m, c_prompt, c_sample,
           ada_w, ada_b, norm1_g, norm2_g, w_in, mlstm_if_b, conv_dw, conv_db, conv_ln_g, conv_ln_b,
           w_conv_out, mlstm_norm_g, w_mlstm_out, w_o, peer_wq, peer_subkeys, peer_u, peer_v, final_g):
    bsz, seq, d = x_prompt.shape
    nb, dseq, _ = x_sample.shape
    assert dseq == 1
    assert ada_w.shape[0] == 1, "single-layer stack only"
    n_heads = mlstm_if_b.shape[1] // 2
    d_ml = mlstm_norm_g.shape[1]
    col_if = 2 * d + 4 * d_ml
    xp = x_prompt.reshape(bsz * seq, d)
    xs = x_sample.reshape(nb, d)
    n_c = nb + bsz
    c_all = jnp.pad(jnp.concatenate([c_sample, c_prompt], axis=0), ((0, (-n_c) % 16), (0, 0)))
    mod = _ada(c_all, ada_w[0], ada_b[0])
    mod_s = mod[:nb].reshape(1, nb, 6 * d)
    mod_p = mod[nb:n_c].reshape(bsz, 1, 6 * d)
    w_in_t = w_in[0].T
    w_if_t = jnp.pad(w_in_t[col_if:col_if + 2 * n_heads], ((0, LANES - 2 * n_heads), (0, 0)))
    w_gate_t = w_in_t[col_if + 2 * n_heads:]
    w = (norm1_g[0], norm2_g[0], w_in_t, w_if_t, w_gate_t, mlstm_if_b[0], conv_dw[0], conv_db[0], conv_ln_g[0],
         conv_ln_b[0], w_conv_out[0], mlstm_norm_g[0], w_mlstm_out[0], w_o[0], peer_wq[0], peer_subkeys[0],
         peer_u[0].astype(BF16), peer_v[0].astype(BF16))
    xp, pp, bp, cp, np_, mp = _layer(xp, mod_p, seq, None, w, n_heads, (bsz, seq))
    xs, ps, bs, cs, ns, ms = _layer(xs, mod_s, nb,
                                    (state_conv[0].transpose(1, 0, 2), state_mlstm_c[0], state_mlstm_n[0],
                                     state_mlstm_m[0]),
                                    w, n_heads, None)
    y_prompt = _final(xp, pp, mod_p, 5, final_g, seq, min(512, seq)).reshape(bsz, seq, d)
    y_sample = _final(xs, ps, mod_s, 5, final_g, nb, nb).reshape(nb, 1, d)
    return (y_prompt, y_sample) + tuple(o[None] for o in (bp, bs, cp, cs, np_, ns, mp, ms))
```

```python
import functools

import jax
import jax.numpy as jnp
from jax import lax
from jax.experimental import pallas as pl
from jax.experimental.pallas import tpu as pltpu

F32 = jnp.float32
BF16 = jnp.bfloat16
I32 = jnp.int32

EPS = 1e-6
NEG_INIT = -1e30
PEER_TOPK = 16
LANES = 128
VMEM_LIMIT = 56 * 1024 * 1024


def _cparams(*sem):
    return pltpu.CompilerParams(dimension_semantics=sem, vmem_limit_bytes=VMEM_LIMIT)


def _tile(target, *extents):
    t = min([target] + [e for e in extents if e])
    while any(e % t for e in extents):
        t -= LANES
    return t


def _sigmoid(x):
    return 1.0 / (1.0 + jnp.exp(-x))


def _log_sigmoid(x):
    return jnp.minimum(x, 0.0) - jnp.log1p(jnp.exp(-jnp.abs(x)))


def _split3(x):
    hi = x.astype(BF16)
    r1 = x - hi.astype(F32)
    mid = r1.astype(BF16)
    lo = (r1 - mid.astype(F32)).astype(BF16)
    return hi, mid, lo


def _ada_kernel(c_ref, w_ref, b_ref, o_ref):
    c = c_ref[...]
    a = (c * _sigmoid(c)).astype(BF16)
    o_ref[...] = jnp.dot(a, w_ref[...].astype(BF16), preferred_element_type=F32) + b_ref[...]


def _ada(c, w, b):
    m, d = c.shape
    n = w.shape[1]
    tn = _tile(1024, n)
    return pl.pallas_call(
        _ada_kernel,
        grid=(n // tn,),
        in_specs=[pl.BlockSpec((m, d), lambda j: (0, 0)),
                  pl.BlockSpec((d, tn), lambda j: (0, j)),
                  pl.BlockSpec((1, tn), lambda j: (0, j))],
        out_specs=pl.BlockSpec((m, tn), lambda j: (0, j)),
        out_shape=jax.ShapeDtypeStruct((m, n), F32),
        compiler_params=_cparams("arbitrary"),
        name="ada_mod",
    )(c, w, b.reshape(1, n))


def _norm_mod_kernel(x_ref, g_ref, sc_ref, sh_ref, o_ref):
    x = x_ref[...]
    y = x * lax.rsqrt(jnp.mean(x * x, axis=-1, keepdims=True) + EPS)
    o_ref[...] = (y * g_ref[...] * (1.0 + sc_ref[0]) + sh_ref[0]).astype(o_ref.dtype)


def _norm_mod(x, g, mod3, sh_idx, sc_idx, grp_rows, tm):
    t, d = x.shape
    r = mod3.shape[1]
    per = grp_rows // tm
    return pl.pallas_call(
        _norm_mod_kernel,
        grid=(t // tm,),
        in_specs=[pl.BlockSpec((tm, d), lambda i: (i, 0)),
                  pl.BlockSpec((1, d), lambda i: (0, 0)),
                  pl.BlockSpec((1, r, d), lambda i: (i // per, 0, sc_idx)),
                  pl.BlockSpec((1, r, d), lambda i: (i // per, 0, sh_idx))],
        out_specs=pl.BlockSpec((tm, d), lambda i: (i, 0)),
        out_shape=jax.ShapeDtypeStruct((t, d), BF16),
        compiler_params=_cparams("arbitrary"),
        name="norm_mod",
    )(x, g.reshape(1, d), mod3, mod3)


def _load_weight_tile(w_ref, transposed):
    w = w_ref[...]
    return (w.T if transposed else w).astype(BF16)


def _mm_kernel(*refs, act, has_mul, has_add, has_res, w_transposed):
    a_ref, w_ref = refs[0], refs[1]
    o_ref, wb_ref = refs[-2], refs[-1]
    extra = list(refs[2:-2])

    @pl.when(pl.program_id(1) == 0)
    def _():
        wb_ref[...] = _load_weight_tile(w_ref, w_transposed)

    acc = jnp.dot(a_ref[...], wb_ref[...], preferred_element_type=F32)
    if act == "sigmoid":
        acc = _sigmoid(acc)
    if has_mul:
        acc = acc * extra.pop(0)[...]
    if has_add:
        acc = acc + extra.pop(0)[...]
    if has_res:
        x_ref, g_ref = extra
        acc = x_ref[...] + g_ref[0] * acc
    o_ref[...] = acc.astype(o_ref.dtype)


def _mm(a, w, *, col0, ncols, tm, tn, out_dtype, act=None, mul=None, add=None, res=None, w_transposed=False,
        name="mm"):
    m, k = a.shape
    tm = min(tm, m)
    tn = _tile(tn, ncols, col0, mul[1] if mul else 0)
    cb = col0 // tn
    nj = ncols // tn
    w_spec = (pl.BlockSpec((tn, k), lambda j, i: (cb + j, 0)) if w_transposed
              else pl.BlockSpec((k, tn), lambda j, i: (0, cb + j)))
    in_specs = [pl.BlockSpec((tm, k), lambda j, i: (i, 0)), w_spec]
    operands = [a, w]
    tile = pl.BlockSpec((tm, tn), lambda j, i: (i, j))
    if mul is not None:
        mb = mul[1] // tn
        in_specs.append(pl.BlockSpec((tm, tn), lambda j, i: (i, mb + j)))
        operands.append(mul[0])
    if add is not None:
        in_specs.append(tile)
        operands.append(add)
    if res is not None:
        x, mod3, g_idx, grp_rows = res
        per = grp_rows // tm
        in_specs += [tile, pl.BlockSpec((1, mod3.shape[1], tn), lambda j, i: (i // per, 0, g_idx * nj + j))]
        operands += [x, mod3]
    return pl.pallas_call(
        functools.partial(_mm_kernel, act=act, has_mul=mul is not None, has_add=add is not None,
                          has_res=res is not None, w_transposed=w_transposed),
        grid=(nj, m // tm),
        in_specs=in_specs,
        out_specs=tile,
        out_shape=jax.ShapeDtypeStruct((m, ncols), out_dtype),
        scratch_shapes=[pltpu.VMEM((k, tn), BF16)],
        compiler_params=_cparams("arbitrary", "arbitrary"),
        name=name,
    )(*operands)


def _mm_glu_kernel(a_ref, wa_ref, wg_ref, o_ref, wab_ref, wgb_ref):
    @pl.when(pl.program_id(1) == 0)
    def _():
        wab_ref[...] = _load_weight_tile(wa_ref, True)
        wgb_ref[...] = _load_weight_tile(wg_ref, True)

    a = a_ref[...]
    lin = jnp.dot(a, wab_ref[...], preferred_element_type=F32)
    gate = jnp.dot(a, wgb_ref[...], preferred_element_type=F32)
    o_ref[...] = lin * _sigmoid(gate)


def _mm_glu(a, w_t, *, d_half, tm, tn):
    m, k = a.shape
    tm = min(tm, m)
    tn = _tile(tn, d_half)
    off = d_half // tn
    return pl.pallas_call(
        _mm_glu_kernel,
        grid=(d_half // tn, m // tm),
        in_specs=[pl.BlockSpec((tm, k), lambda j, i: (i, 0)),
                  pl.BlockSpec((tn, k), lambda j, i: (j, 0)),
                  pl.BlockSpec((tn, k), lambda j, i: (off + j, 0))],
        out_specs=pl.BlockSpec((tm, tn), lambda j, i: (i, j)),
        out_shape=jax.ShapeDtypeStruct((m, d_half), F32),
        scratch_shapes=[pltpu.VMEM((k, tn), BF16), pltpu.VMEM((k, tn), BF16)],
        compiler_params=_cparams("arbitrary", "arbitrary"),
        name="mm_glu",
    )(a, w_t, w_t)


_HALO = 32
_CONV_RC = 64


def _ln_silu(y, lg, lb):
    mu = jnp.mean(y, axis=-1, keepdims=True)
    yc = y - mu
    var = jnp.mean(yc * yc, axis=-1, keepdims=True)
    z = yc * lax.rsqrt(var + EPS) * lg + lb
    return z * _sigmoid(z)


def _conv_prompt_kernel(u_ref, dw_ref, db_ref, lg_ref, lb_ref, o_ref, win_ref, y_ref, sh_ref, *, tt, kw):
    d = u_ref.shape[2]
    t = pl.program_id(1)

    @pl.when(t == 0)
    def _():
        win_ref[0:_HALO, :] = jnp.zeros((_HALO, d), F32)

    @pl.when(t > 0)
    def _():
        win_ref[0:_HALO, :] = win_ref[tt:tt + _HALO, :]

    win_ref[_HALO:_HALO + tt, :] = u_ref[0]
    base = _HALO - (kw - 1)
    rc = min(_CONV_RC, tt)
    for mis in range(1, 8):
        sh_ref[mis - 1] = win_ref[mis:mis + tt + _HALO - 8, :]

    def row_chunk(r, carry):
        r0 = pl.multiple_of(r * rc, rc)
        for c in range(d // LANES):
            ls = slice(c * LANES, (c + 1) * LANES)
            acc = jnp.broadcast_to(db_ref[:, ls], (rc, LANES))
            for mis in range(8):
                taps = [j for j in range(kw) if (base + j) % 8 == mis]
                span = max(base + j - mis for j in taps) + rc
                w = win_ref[pl.ds(r0, span), ls] if mis == 0 else sh_ref[mis - 1, pl.ds(r0, span), ls]
                for j in taps:
                    off = base + j - mis
                    acc = acc + w[off:off + rc, :] * dw_ref[j:j + 1, ls]
            y_ref[pl.ds(r0, rc), ls] = acc
        return carry

    lax.fori_loop(0, tt // rc, row_chunk, 0)
    o_ref[0] = _ln_silu(y_ref[...], lg_ref[...], lb_ref[...]).astype(o_ref.dtype)


def _conv_prompt(u3, dw, db, lg, lb, tt):
    b, t, d = u3.shape
    tt = min(tt, t)
    kw = dw.shape[0]
    assert kw - 1 <= _HALO <= tt
    return pl.pallas_call(
        functools.partial(_conv_prompt_kernel, tt=tt, kw=kw),
        grid=(b, t // tt),
        in_specs=[pl.BlockSpec((1, tt, d), lambda i, j: (i, j, 0)),
                  pl.BlockSpec((kw, d), lambda i, j: (0, 0)),
                  pl.BlockSpec((1, d), lambda i, j: (0, 0)),
                  pl.BlockSpec((1, d), lambda i, j: (0, 0)),
                  pl.BlockSpec((1, d), lambda i, j: (0, 0))],
        out_specs=pl.BlockSpec((1, tt, d), lambda i, j: (i, j, 0)),
        out_shape=jax.ShapeDtypeStruct((b, t, d), BF16),
        scratch_shapes=[pltpu.VMEM((_HALO + tt, d), F32), pltpu.VMEM((tt, d), F32),
                        pltpu.VMEM((7, tt + _HALO - 8, d), F32)],
        compiler_params=_cparams("arbitrary", "arbitrary"),
        name="conv_prompt",
    )(u3, dw, db.reshape(1, d), lg.reshape(1, d), lb.reshape(1, d))


def _conv_sample_kernel(buf_ref, u_ref, dw_ref, db_ref, lg_ref, lb_ref, o_ref, nb_ref, *, kw):
    u = u_ref[...]
    acc = u * dw_ref[kw - 1:kw, :] + db_ref[...]
    for j in range(kw - 1):
        acc = acc + buf_ref[j] * dw_ref[j:j + 1, :]
    o_ref[...] = _ln_silu(acc, lg_ref[...], lb_ref[...]).astype(o_ref.dtype)
    for j in range(kw - 2):
        nb_ref[j] = buf_ref[j + 1]
    nb_ref[kw - 2] = u


def _conv_sample(buf_t, u, dw, db, lg, lb, bb=16):
    b, d = u.shape
    kw = dw.shape[0]
    bb = min(bb, b)
    return pl.pallas_call(
        functools.partial(_conv_sample_kernel, kw=kw),
        grid=(b // bb,),
        in_specs=[pl.BlockSpec((kw - 1, bb, d), lambda i: (0, i, 0)),
                  pl.BlockSpec((bb, d), lambda i: (i, 0)),
                  pl.BlockSpec((kw, d), lambda i: (0, 0)),
                  pl.BlockSpec((1, d), lambda i: (0, 0)),
                  pl.BlockSpec((1, d), lambda i: (0, 0)),
                  pl.BlockSpec((1, d), lambda i: (0, 0))],
        out_specs=[pl.BlockSpec((bb, d), lambda i: (i, 0)),
                   pl.BlockSpec((kw - 1, bb, d), lambda i: (0, i, 0))],
        out_shape=[jax.ShapeDtypeStruct((b, d), BF16),
                   jax.ShapeDtypeStruct((kw - 1, b, d), F32)],
        compiler_params=_cparams("arbitrary"),
        name="conv_sample",
    )(buf_t, u, dw, db.reshape(1, d), lg.reshape(1, d), lb.reshape(1, d))


_MLSTM_CB = 256


def _mlstm_prompt_kernel(bias_ref, q_ref, k_ref, v_ref, ifc_ref, ifr_ref, og_ref, ng_ref,
                         h_ref, c_ref, n_ref, m_ref, cb_ref, ns_ref, ms_ref, *, n_heads):
    hd = pl.program_id(1)
    ck = pl.program_id(2)
    L, dk = q_ref.shape
    dv = v_ref.shape[1]
    scale = dk ** -0.5

    @pl.when(ck == 0)
    def _():
        c_ref[...] = jnp.zeros(c_ref.shape, F32)
        cb_ref[...] = jnp.zeros(cb_ref.shape, BF16)
        ns_ref[...] = jnp.zeros(ns_ref.shape, F32)
        ms_ref[...] = jnp.full(ms_ref.shape, NEG_INIT, F32)

    q = q_ref[...]
    k = k_ref[...]
    v = v_ref[...]
    bi = bias_ref[hd]
    bf = bias_ref[n_heads + hd]
    col = ifc_ref[0, 0]
    li_c = col[:, 0:1] + bi
    lf_c = _log_sigmoid(col[:, 1:2] + bf)
    row = ifr_ref[0, 0]
    li_r = row[0:1, :] + bi
    lf_r = _log_sigmoid(row[1:2, :] + bf)

    r_io = lax.broadcasted_iota(I32, (L, L), 0)
    c_io = lax.broadcasted_iota(I32, (L, L), 1)
    causal = r_io >= c_io
    tri = jnp.where(causal, 1.0, 0.0).astype(BF16)
    tri_t = jnp.where(r_io <= c_io, 1.0, 0.0).astype(BF16)
    b_c = sum(jnp.dot(tri, p, preferred_element_type=F32)
              for p in _split3(jnp.broadcast_to(lf_c, (L, LANES))))[:, 0:1]
    b_r = sum(jnp.dot(p, tri_t, preferred_element_type=F32)
              for p in _split3(jnp.broadcast_to(lf_r, (16, L))))[0:1, :]

    m_prev = ms_ref[0:1, 0:1]
    b_last = b_c[L - 1:L, :]
    dmat = jnp.where(causal, b_c - (b_r - li_r), -jnp.inf)
    inter = b_c + m_prev
    m_t = jnp.maximum(inter, jnp.max(dmat, axis=-1, keepdims=True))
    w_inter = jnp.exp(inter - m_t)
    s = lax.dot_general(q, k, (((1,), (1,)), ((), ())), preferred_element_type=F32)
    s = s * scale * jnp.exp(dmat - m_t)
    qc = jnp.dot(q, cb_ref[...], preferred_element_type=F32)
    num = w_inter * qc + jnp.dot(s.astype(BF16), v, preferred_element_type=F32)
    qn = jnp.sum(q.astype(F32) * ns_ref[...], axis=-1, keepdims=True)
    den = w_inter * qn + jnp.sum(s, axis=-1, keepdims=True)
    h = num / jnp.maximum(jnp.abs(den), jnp.exp(-m_t))
    hn = h * lax.rsqrt(jnp.mean(h * h, axis=-1, keepdims=True) + EPS)
    h_ref[...] = (hn * ng_ref[...] * og_ref[...]).astype(h_ref.dtype)

    g_c = b_last - b_c + li_c
    m_new = jnp.maximum(b_last + m_prev, jnp.max(g_c, axis=0, keepdims=True))
    w_s = jnp.exp(g_c - m_new)
    decay = jnp.exp(b_last + m_prev - m_new)
    kw = k.astype(F32) * (w_s * scale)
    kwb = kw.astype(BF16)
    cbw = min(_MLSTM_CB, dv)
    for j in range(dv // cbw):
        cs = slice(j * cbw, (j + 1) * cbw)
        kv = lax.dot_general(kwb, v[:, cs], (((0,), (0,)), ((), ())), preferred_element_type=F32)
        c_new = decay * c_ref[0, 0, :, cs] + kv
        c_ref[0, 0, :, cs] = c_new
        cb_ref[:, cs] = c_new.astype(BF16)
    n_new = decay * ns_ref[...] + jnp.sum(kw, axis=0, keepdims=True)
    ns_ref[...] = n_new
    ms_ref[...] = jnp.broadcast_to(m_new, ms_ref.shape)
    n_ref[0, 0] = n_new
    m_ref[0, 0] = jnp.broadcast_to(m_new, (1, LANES))


def _mlstm_prompt(qkv, og, if_pre, if_b, norm_g, batch, seq, n_heads, chunk):
    dh = qkv.shape[1] // (3 * n_heads)
    chunk = min(chunk, seq)
    nc = seq // chunk
    ifx = if_pre.reshape(batch, seq, 2, n_heads)
    ifc = ifx.transpose(0, 3, 1, 2)
    ifr = ifx.transpose(0, 3, 2, 1)
    row = lambda b, h, c: (b * nc + c)
    h_out, c_out, n_out, m_out = pl.pallas_call(
        functools.partial(_mlstm_prompt_kernel, n_heads=n_heads),
        grid=(batch, n_heads, nc),
        in_specs=[pl.BlockSpec(memory_space=pltpu.SMEM),
                  pl.BlockSpec((chunk, dh), lambda b, h, c: (row(b, h, c), h)),
                  pl.BlockSpec((chunk, dh), lambda b, h, c: (row(b, h, c), n_heads + h)),
                  pl.BlockSpec((chunk, dh), lambda b, h, c: (row(b, h, c), 2 * n_heads + h)),
                  pl.BlockSpec((1, 1, chunk, 2), lambda b, h, c: (b, h, c, 0)),
                  pl.BlockSpec((1, 1, 2, chunk), lambda b, h, c: (b, h, 0, c)),
                  pl.BlockSpec((chunk, dh), lambda b, h, c: (row(b, h, c), h)),
                  pl.BlockSpec((1, dh), lambda b, h, c: (0, h))],
        out_specs=[pl.BlockSpec((chunk, dh), lambda b, h, c: (row(b, h, c), h)),
                   pl.BlockSpec((1, 1, dh, dh), lambda b, h, c: (b, h, 0, 0)),
                   pl.BlockSpec((1, 1, 1, dh), lambda b, h, c: (b, h, 0, 0)),
                   pl.BlockSpec((1, 1, 1, LANES), lambda b, h, c: (b, h, 0, 0))],
        out_shape=[jax.ShapeDtypeStruct((batch * seq, n_heads * dh), BF16),
                   jax.ShapeDtypeStruct((batch, n_heads, dh, dh), F32),
                   jax.ShapeDtypeStruct((batch, n_heads, 1, dh), F32),
                   jax.ShapeDtypeStruct((batch, n_heads, 1, LANES), F32)],
        scratch_shapes=[pltpu.VMEM((dh, dh), BF16), pltpu.VMEM((1, dh), F32), pltpu.VMEM((8, LANES), F32)],
        compiler_params=_cparams("arbitrary", "arbitrary", "arbitrary"),
        name="mlstm_prompt",
    )(if_b, qkv, qkv, qkv, ifc, ifr, og, norm_g.reshape(1, -1))
    return h_out, c_out, n_out[:, :, 0, :], m_out[:, :, 0, 0]


def _mlstm_sample_head_setup(b, q_ref, k_ref, qt_ref, kt_ref):
    @pl.when(b == 0)
    def _():
        for src, dst in ((q_ref, qt_ref), (k_ref, kt_ref)):
            for i, p in enumerate(_split3(src[...].T)):
                dst[i] = p


def _mlstm_sample_columns(b, qt_ref, kt_ref):
    nb = qt_ref.shape[2]
    sel = jnp.where(lax.broadcasted_iota(I32, (nb, LANES), 0) == b, 1.0, 0.0).astype(BF16)
    return tuple(sum(jnp.dot(t_ref[i], sel, preferred_element_type=F32) for i in range(3))
                 for t_ref in (qt_ref, kt_ref))


def _mlstm_sample_item(hd, b, columns, bias_ref, ifp_ref, mst_ref, q_ref, k_ref, v_ref, c_ref, n_ref, og_ref,
                       ng_ref, h_ref, co_ref, no_ref, mo_ref, qt_ref, kt_ref, *, n_heads):
    nb, dk = q_ref.shape
    dv = v_ref.shape[1]
    scale = dk ** -0.5
    qb, kb = columns
    li = jnp.full((1, 1), ifp_ref[b, hd] + bias_ref[hd], F32)
    lf = _log_sigmoid(jnp.full((1, 1), ifp_ref[b, n_heads + hd] + bias_ref[n_heads + hd], F32))
    m_prev = jnp.full((1, 1), mst_ref[b, hd], F32)
    m_t = jnp.maximum(lf + m_prev, li)
    decay = jnp.exp(lf + m_prev - m_t)
    w_s = jnp.exp(li - m_t)

    q_row = q_ref[pl.ds(b, 1), :]
    k_row = k_ref[pl.ds(b, 1), :]
    v_row = v_ref[pl.ds(b, 1), :]
    n_row = n_ref[0, 0]
    sval = jnp.sum(q_row * k_row, axis=-1, keepdims=True) * scale * w_s
    den = decay * jnp.sum(q_row * n_row, axis=-1, keepdims=True) + sval
    denom = jnp.maximum(jnp.abs(den), jnp.exp(-m_t))
    wk = w_s * scale
    pieces = []
    for j in range(dv // LANES):
        ls = slice(j * LANES, (j + 1) * LANES)
        cj = c_ref[0, 0, :, ls]
        vj = v_row[:, ls]
        qc = jnp.sum(qb * cj, axis=0, keepdims=True)
        co_ref[0, 0, :, ls] = decay * cj + kb * (wk * vj)
        pieces.append((decay * qc + sval * vj) / denom)
    h = jnp.concatenate(pieces, axis=1)
    hn = h * lax.rsqrt(jnp.mean(h * h, axis=-1, keepdims=True) + EPS)
    h_ref[0] = (hn * ng_ref[...] * og_ref[0]).astype(h_ref.dtype)
    no_ref[0, 0] = decay * n_row + wk * k_row
    mo_ref[0, 0] = jnp.broadcast_to(m_t, (1, LANES))


def _mlstm_sample_kernel(*refs, n_heads):
    b = pl.program_id(1)
    _mlstm_sample_head_setup(b, refs[3], refs[4], refs[-2], refs[-1])
    _mlstm_sample_item(pl.program_id(0), b, _mlstm_sample_columns(b, refs[-2], refs[-1]), *refs, n_heads=n_heads)


def _mlstm_sample_plumbing(qkv, og, if_pre, if_b, norm_g, st_c, st_n, st_m, n_heads, item):
    nb = qkv.shape[0]
    dh = qkv.shape[1] // (3 * n_heads)

    def at(fn):
        return lambda *ids: fn(*item(*ids))

    smem = pl.BlockSpec(memory_space=pltpu.SMEM)
    in_specs = [smem, smem, smem,
                pl.BlockSpec((nb, dh), at(lambda h, b: (0, h))),
                pl.BlockSpec((nb, dh), at(lambda h, b: (0, n_heads + h))),
                pl.BlockSpec((nb, dh), at(lambda h, b: (0, 2 * n_heads + h))),
                pl.BlockSpec((1, 1, dh, dh), at(lambda h, b: (b, h, 0, 0))),
                pl.BlockSpec((1, 1, 1, dh), at(lambda h, b: (b, h, 0, 0))),
                pl.BlockSpec((1, 1, dh), at(lambda h, b: (b, 0, h))),
                pl.BlockSpec((1, dh), at(lambda h, b: (0, h)))]
    operands = [if_b, if_pre, st_m, qkv, qkv, qkv, st_c, st_n.reshape(nb, n_heads, 1, dh),
                og.reshape(nb, 1, n_heads * dh), norm_g.reshape(1, -1)]
    out_specs = [pl.BlockSpec((1, 1, dh), at(lambda h, b: (b, 0, h))),
                 pl.BlockSpec((1, 1, dh, dh), at(lambda h, b: (b, h, 0, 0))),
                 pl.BlockSpec((1, 1, 1, dh), at(lambda h, b: (b, h, 0, 0))),
                 pl.BlockSpec((1, 1, 1, LANES), at(lambda h, b: (b, h, 0, 0)))]
    out_shape = [jax.ShapeDtypeStruct((nb, 1, n_heads * dh), BF16),
                 jax.ShapeDtypeStruct((nb, n_heads, dh, dh), F32),
                 jax.ShapeDtypeStruct((nb, n_heads, 1, dh), F32),
                 jax.ShapeDtypeStruct((nb, n_heads, 1, LANES), F32)]
    scratch = [pltpu.VMEM((3, dh, nb), BF16), pltpu.VMEM((3, dh, nb), BF16)]
    return operands, in_specs, out_specs, out_shape, scratch


def _mlstm_sample_results(h_out, c_out, n_out, m_out):
    return h_out.reshape(h_out.shape[0], -1), c_out, n_out[:, :, 0, :], m_out[:, :, 0, 0]


def _mlstm_sample(qkv, og, if_pre, if_b, norm_g, st_c, st_n, st_m, n_heads):
    operands, in_specs, out_specs, out_shape, scratch = _mlstm_sample_plumbing(
        qkv, og, if_pre, if_b, norm_g, st_c, st_n, st_m, n_heads, lambda h, b: (h, b))
    outs = pl.pallas_call(
        functools.partial(_mlstm_sample_kernel, n_heads=n_heads),
        grid=(n_heads, qkv.shape[0]),
        in_specs=in_specs, out_specs=out_specs, out_shape=out_shape, scratch_shapes=scratch,
        compiler_params=_cparams("arbitrary", "arbitrary"),
        name="mlstm_sample",
    )(*operands)
    return _mlstm_sample_results(*outs)


def _topk_rows(problems, n_take):
    state = []
    for s, payload in problems:
        iota = lax.broadcasted_iota(I32, s.shape, 0).astype(F32)
        state.append([s, iota, payload, [], []])
    for _ in range(n_take):
        for st in state:
            s, iota, payload, vals, picks = st
            m = jnp.max(s, axis=0, keepdims=True)
            pos = jnp.min(jnp.where(s == m, iota, float(s.shape[0])), axis=0, keepdims=True)
            hit = iota == pos
            vals.append(m)
            picks.append(pos if payload is None
                         else jnp.max(jnp.where(hit, payload, -1.0), axis=0, keepdims=True))
            st[0] = jnp.where(hit, -jnp.inf, s)
    return [(st[3], st[4]) for st in state]


def _peer_pairs(topk):
    return [(i, j) for i in range(topk) for j in range(topk) if (i + 1) * (j + 1) <= topk]


def _peer_topk_kernel(q_ref, sk_ref, a_ref, b_ref, g_ref, cv_ref, ce_ref, *, topk):
    tb = q_ref.shape[0]
    nk, dsub = sk_ref.shape[2], sk_ref.shape[3]
    pairs = _peer_pairs(topk)
    sk = [sk_ref[0, p].astype(BF16) for p in range(2)]

    n_groups = tb // LANES
    per_iter = cv_ref.shape[0]

    def groups(gi, carry):
        starts = [pl.multiple_of((gi * per_iter + n) * LANES, LANES) for n in range(per_iter)]
        for n, t0 in enumerate(starts):
            scores = []
            for p in range(2):
                qp = q_ref[pl.ds(t0, LANES), p * dsub:(p + 1) * dsub].astype(BF16)
                scores.append(lax.dot_general(sk[p], qp, (((1,), (1,)), ((), ())),
                                              preferred_element_type=F32))
            (sv0, si0), (sv1, si1) = _topk_rows([(scores[0], None), (scores[1], None)], topk)
            cv_ref[n] = jnp.full(cv_ref.shape[1:], -jnp.inf, F32)
            ce_ref[n] = jnp.zeros(ce_ref.shape[1:], F32)
            for r, (i, j) in enumerate(pairs):
                cv_ref[n, r:r + 1, :] = sv0[i] + sv1[j]
                ce_ref[n, r:r + 1, :] = si0[i] * float(nk) + si1[j]
        picked = _topk_rows([(cv_ref[n], ce_ref[n]) for n in range(per_iter)], topk)
        for t0, (tv, te) in zip(starts, picked):
            ex = [jnp.exp(x - tv[0]) for x in tv]
            tot = sum(ex)
            for r in range(topk):
                first = jnp.floor(te[r] * (1.0 / nk))
                a_ref[r:r + 1, pl.ds(t0, LANES)] = first.astype(I32)
                b_ref[r:r + 1, pl.ds(t0, LANES)] = (te[r] - first * float(nk)).astype(I32)
                g_ref[r:r + 1, pl.ds(t0, LANES)] = ex[r] / tot
        return carry

    lax.fori_loop(0, n_groups // per_iter, groups, 0)


def _peer_topk(q, subkeys, topk, tb):
    t = q.shape[0]
    nh, _, nk, dsub = subkeys.shape
    tb = min(tb, t)
    assert nk & (nk - 1) == 0 and nk * nk < 2 ** 24
    n_pad = -(-len(_peer_pairs(topk)) // 8) * 8
    per_iter = 2 if (tb // LANES) % 2 == 0 else 1
    out = jax.ShapeDtypeStruct((nh * topk, t), I32)
    return pl.pallas_call(
        functools.partial(_peer_topk_kernel, topk=topk),
        grid=(t // tb, nh),
        in_specs=[pl.BlockSpec((tb, 2 * dsub), lambda i, h: (i, h)),
                  pl.BlockSpec((1, 2, nk, dsub), lambda i, h: (h, 0, 0, 0))],
        out_specs=[pl.BlockSpec((topk, tb), lambda i, h: (h, i))] * 3,
        out_shape=[out, out, jax.ShapeDtypeStruct((nh * topk, t), F32)],
        scratch_shapes=[pltpu.VMEM((per_iter, n_pad, LANES), F32)] * 2,
        compiler_params=_cparams("arbitrary", "arbitrary"),
        name="peer_topk",
    )(q, subkeys)


def _peer_gates_kernel(a_ref, b_ref, g_ref, o_ref, at_ref, bt_ref, gt_ref):
    nk = o_ref.shape[1]
    tg = o_ref.shape[0]
    at_ref[...] = a_ref[...].T
    bt_ref[...] = b_ref[...].T
    gt_ref[...] = g_ref[...].T
    key = lax.broadcasted_iota(I32, (nk, at_ref.shape[1]), 0)

    def token(t, carry):
        a = at_ref[pl.ds(t, 1), :]
        b = bt_ref[pl.ds(t, 1), :]
        g = gt_ref[pl.ds(t, 1), :]
        first = jnp.where(key == a, g, 0.0).astype(BF16)
        second = jnp.where(key == b, 1.0, 0.0).astype(BF16)
        o_ref[t] = lax.dot_general(first, second, (((1,), (1,)), ((), ())), preferred_element_type=F32)
        return carry

    lax.fori_loop(0, tg, token, 0, unroll=8)


def _peer_gates(a_idx, b_idx, gates, nk, tg):
    hk, t = a_idx.shape
    tg = min(tg, t)
    return pl.pallas_call(
        _peer_gates_kernel,
        grid=(t // tg,),
        in_specs=[pl.BlockSpec((hk, tg), lambda i: (0, i))] * 3,
        out_specs=pl.BlockSpec((tg, nk, nk), lambda i: (i, 0, 0)),
        out_shape=jax.ShapeDtypeStruct((t, nk, nk), F32),
        scratch_shapes=[pltpu.VMEM((tg, hk), I32), pltpu.VMEM((tg, hk), I32), pltpu.VMEM((tg, hk), F32)],
        compiler_params=_cparams("arbitrary"),
        name="peer_gates",
    )(a_idx, b_idx, gates)


_SQRT_HALF = 0.7071067811865476


_G_KEYS = 8
_N_RIDER_IN = 10
_N_RIDER_OUT = 4


def _peer_dense_kernel(*refs, n_blocks, keys_per_step, rider):
    x_ref, u_ref, v_ref, g_ref = refs[:4]
    if rider is None:
        o_ref, coef_ref = refs[4:]
    else:
        rider_in = refs[4:4 + _N_RIDER_IN]
        o_ref = refs[4 + _N_RIDER_IN]
        rider_out = refs[5 + _N_RIDER_IN:5 + _N_RIDER_IN + _N_RIDER_OUT]
        coef_ref = refs[5 + _N_RIDER_IN + _N_RIDER_OUT]
        rider_scratch = refs[6 + _N_RIDER_IN + _N_RIDER_OUT:]
    e = pl.program_id(1)
    cur = e % 2
    prev = 1 - cur

    @pl.when(e == 0)
    def _():
        o_ref[...] = jnp.zeros(o_ref.shape, F32)
        coef_ref[1] = jnp.zeros(coef_ref.shape[1:], BF16)

    if rider is not None:
        n_heads, n_items, nb = rider
        item = jnp.minimum(pl.program_id(0) * (n_blocks + 1) + e, n_items - 1)
        _mlstm_sample_head_setup(item % nb, rider_in[3], rider_in[4], *rider_scratch)
        columns = _mlstm_sample_columns(item % nb, *rider_scratch)

    a = lax.dot_general(x_ref[...], u_ref[...], (((1,), (1,)), ((), ())), preferred_element_type=F32)
    o_ref[...] += jnp.dot(coef_ref[prev], v_ref[...], preferred_element_type=F32)
    act = 0.5 * a * (1.0 + lax.erf(a * _SQRT_HALF))
    per_block = g_ref.shape[1] // keys_per_step
    first = 0 if per_block == 1 else (jnp.minimum(e, n_blocks - 1) % per_block) * keys_per_step
    g = jnp.concatenate([g_ref[:, first + s, :] for s in range(keys_per_step)], axis=1)
    coef_ref[cur] = (g * act).astype(BF16)
    if rider is not None:
        _mlstm_sample_item(item // nb, item % nb, columns, *rider_in, *rider_out, *rider_scratch, n_heads=n_heads)


def _peer_dense(x, u_tab, v_tab, gmat, tb, keys_per_step, rider_args=None):
    t, d = x.shape
    nk = gmat.shape[1]
    tb = min(tb, t)
    eb = keys_per_step * nk
    ne = nk // keys_per_step
    per_block = _G_KEYS // keys_per_step
    cur_blk = lambda e: jnp.minimum(e, ne - 1)
    in_specs = [pl.BlockSpec((tb, d), lambda i, e: (i, 0)),
                pl.BlockSpec((eb, d), lambda i, e: (cur_blk(e), 0)),
                pl.BlockSpec((eb, d), lambda i, e: (jnp.maximum(e - 1, 0), 0)),
                pl.BlockSpec((tb, _G_KEYS, nk), lambda i, e: (i, cur_blk(e) // per_block, 0))]
    operands = [x, u_tab, v_tab, gmat]
    out_specs = [pl.BlockSpec((tb, d), lambda i, e: (i, 0))]
    out_shape = [jax.ShapeDtypeStruct((t, d), F32)]
    scratch = [pltpu.VMEM((2, tb, eb), BF16)]
    rider = None
    if rider_args is not None:
        n_heads = rider_args[-1]
        nb = rider_args[0].shape[0]
        n_items = n_heads * nb
        assert (t // tb) * (ne + 1) >= n_items

        def item(i, e):
            k = jnp.minimum(i * (ne + 1) + e, n_items - 1)
            return k // nb, k % nb

        r_ops, r_in, r_out, r_shape, r_scratch = _mlstm_sample_plumbing(*rider_args, item)
        assert len(r_ops) == _N_RIDER_IN and len(r_out) == _N_RIDER_OUT
        operands += r_ops
        in_specs += r_in
        out_specs += r_out
        out_shape += r_shape
        scratch += r_scratch
        rider = (n_heads, n_items, nb)
    outs = pl.pallas_call(
        functools.partial(_peer_dense_kernel, n_blocks=ne, keys_per_step=keys_per_step, rider=rider),
        grid=(t // tb, ne + 1),
        in_specs=in_specs, out_specs=out_specs, out_shape=out_shape, scratch_shapes=scratch,
        compiler_params=_cparams("arbitrary", "arbitrary"),
        name="peer_dense" if rider is None else "peer_dense_mlstm_sample",
    )(*operands)
    return outs[0], (None if rider is None else _mlstm_sample_results(*outs[1:]))


def _final_kernel(x_ref, p_ref, g_ref, fg_ref, o_ref):
    x = x_ref[...] + g_ref[0] * p_ref[...]
    o_ref[...] = x * lax.rsqrt(jnp.mean(x * x, axis=-1, keepdims=True) + EPS) * fg_ref[...]


def _final(x, p, mod3, g_idx, final_g, grp_rows, tm):
    t, d = x.shape
    r = mod3.shape[1]
    per = grp_rows // tm
    return pl.pallas_call(
        _final_kernel,
        grid=(t // tm,),
        in_specs=[pl.BlockSpec((tm, d), lambda i: (i, 0)),
                  pl.BlockSpec((tm, d), lambda i: (i, 0)),
                  pl.BlockSpec((1, r, d), lambda i: (i // per, 0, g_idx)),
                  pl.BlockSpec((1, d), lambda i: (0, 0))],
        out_specs=pl.BlockSpec((tm, d), lambda i: (i, 0)),
        out_shape=jax.ShapeDtypeStruct((t, d), F32),
        compiler_params=_cparams("arbitrary"),
        name="final_norm",
    )(x, p, mod3, final_g.reshape(1, d))


_PEER_TB = 512
_PEER_KEYS_PER_STEP = 4


def _in_proj(x, mod3, grp_rows, w, n_heads, qkv_dtype):
    t, d = x.shape
    d_ml = w["mlstm_norm_g"].shape[0]
    tm = min(512, grp_rows, t)
    h1 = _norm_mod(x, w["norm1_g"], mod3, 0, 1, grp_rows, tm)
    u = _mm_glu(h1, w["w_in_t"], d_half=d, tm=tm, tn=512)
    proj = functools.partial(_mm, h1, tm=tm, w_transposed=True)
    qkv = proj(w["w_in_t"], col0=2 * d, ncols=3 * d_ml, tn=1024, out_dtype=qkv_dtype, name="mm_qkv")
    og = proj(w["w_in_t"], col0=2 * d + 3 * d_ml, ncols=d_ml, tn=1024, out_dtype=F32, act="sigmoid",
              name="mm_ogate")
    if_pre = proj(w["w_if_t"], col0=0, ncols=LANES, tn=LANES, out_dtype=F32, name="mm_if")[:, :2 * n_heads]
    gates = proj(w["w_gate_t"], col0=0, ncols=2 * d, tn=1024, out_dtype=F32, act="sigmoid", name="mm_gates")
    return u, qkv, og, if_pre, gates


def _mix_and_route(x, mod3, grp_rows, w, y, hm, gates):
    t, d = x.shape
    tm = min(512, grp_rows, t)
    conv_g = _mm(y, w["w_conv_out"], col0=0, ncols=d, tm=tm, tn=1024, out_dtype=F32, mul=(gates, 0),
                 name="mm_conv_out")
    merged = _mm(hm, w["w_mlstm_out"], col0=0, ncols=d, tm=tm, tn=512, out_dtype=BF16, mul=(gates, d), add=conv_g,
                 name="mm_mlstm_out")
    x1 = _mm(merged, w["w_o"], col0=0, ncols=d, tm=tm, tn=1024, out_dtype=F32, res=(x, mod3, 2, grp_rows),
             name="mm_merge")
    h2 = _norm_mod(x1, w["norm2_g"], mod3, 3, 4, grp_rows, tm)
    pq = _mm(h2, w["peer_wq"], col0=0, ncols=w["peer_wq"].shape[1], tm=tm, tn=1024, out_dtype=F32, name="mm_peer_q")
    a_idx, b_idx, pg = _peer_topk(pq, w["peer_subkeys"], PEER_TOPK, tb=_PEER_TB)
    gmat = _peer_gates(a_idx, b_idx, pg, w["peer_subkeys"].shape[2], tg=LANES)
    return x1, h2, gmat


def kernel(x_prompt, x_sample, state_conv, state_mlstm_c, state_mlstm_n, state_mlstm_m, c_prompt, c_sample,
           ada_w, ada_b, norm1_g, norm2_g, w_in, mlstm_if_b, conv_dw, conv_db, conv_ln_g, conv_ln_b,
           w_conv_out, mlstm_norm_g, w_mlstm_out, w_o, peer_wq, peer_subkeys, peer_u, peer_v, final_g):
    bsz, seq, d = x_prompt.shape
    nb, dseq, _ = x_sample.shape
    assert dseq == 1
    assert ada_w.shape[0] == 1, "single-layer stack only"
    n_heads = mlstm_if_b.shape[1] // 2
    d_ml = mlstm_norm_g.shape[1]
    col_if = 2 * d + 4 * d_ml
    xp = x_prompt.reshape(bsz * seq, d)
    xs = x_sample.reshape(nb, d)
    n_c = nb + bsz
    c_all = jnp.pad(jnp.concatenate([c_sample, c_prompt], axis=0), ((0, (-n_c) % 16), (0, 0)))
    mod = _ada(c_all, ada_w[0], ada_b[0])
    mod_s = mod[:nb].reshape(1, nb, 6 * d)
    mod_p = mod[nb:n_c].reshape(bsz, 1, 6 * d)
    w_in_t = w_in[0].T
    w_if_t = jnp.pad(w_in_t[col_if:col_if + 2 * n_heads], ((0, LANES - 2 * n_heads), (0, 0)))
    w_gate_t = w_in_t[col_if + 2 * n_heads:]
    w = dict(norm1_g=norm1_g[0], norm2_g=norm2_g[0], w_in_t=w_in_t, w_if_t=w_if_t, w_gate_t=w_gate_t,
             w_conv_out=w_conv_out[0], mlstm_norm_g=mlstm_norm_g[0], w_mlstm_out=w_mlstm_out[0], w_o=w_o[0],
             peer_wq=peer_wq[0], peer_subkeys=peer_subkeys[0])
    conv_w = (conv_dw[0], conv_db[0], conv_ln_g[0], conv_ln_b[0])
    u_tab, v_tab = peer_u[0].astype(BF16), peer_v[0].astype(BF16)
    if_b, ml_g = mlstm_if_b[0], mlstm_norm_g[0]

    us, qkv_s, og_s, if_s, gates_s = _in_proj(xs, mod_s, nb, w, n_heads, F32)
    ys, buf_s_t = _conv_sample(state_conv[0].transpose(1, 0, 2), us, *conv_w)
    bs = buf_s_t.transpose(1, 0, 2)
    sample_step = (qkv_s, og_s, if_s, if_b, ml_g, state_mlstm_c[0], state_mlstm_n[0], state_mlstm_m[0], n_heads)

    up, qkv_p, og_p, if_p, gates_p = _in_proj(xp, mod_p, seq, w, n_heads, BF16)
    up3 = up.reshape(bsz, seq, d)
    yp = _conv_prompt(up3, *conv_w, tt=256).reshape(bsz * seq, d)
    bp = up3[:, seq - (conv_dw.shape[1] - 1):, :]
    hm_p, cp, np_, mp = _mlstm_prompt(qkv_p, og_p, if_p, if_b, ml_g, bsz, seq, n_heads, 256)
    xp, h2_p, gmat_p = _mix_and_route(xp, mod_p, seq, w, yp, hm_p, gates_p)

    n_steps = (bsz * seq // min(_PEER_TB, bsz * seq)) * (peer_subkeys.shape[3] // _PEER_KEYS_PER_STEP + 1)
    if n_steps >= n_heads * nb:
        pp, (hm_s, cs, ns, ms) = _peer_dense(h2_p, u_tab, v_tab, gmat_p, _PEER_TB, _PEER_KEYS_PER_STEP, sample_step)
    else:
        pp, _ = _peer_dense(h2_p, u_tab, v_tab, gmat_p, _PEER_TB, _PEER_KEYS_PER_STEP)
        hm_s, cs, ns, ms = _mlstm_sample(*sample_step)

    xs, h2_s, gmat_s = _mix_and_route(xs, mod_s, nb, w, ys, hm_s, gates_s)
    ps, _ = _peer_dense(h2_s, u_tab, v_tab, gmat_s, _PEER_TB, _G_KEYS)
    y_prompt = _final(xp, pp, mod_p, 5, final_g, seq, min(512, seq)).reshape(bsz, seq, d)
    y_sample = _final(xs, ps, mod_s, 5, final_g, nb, nb).reshape(nb, 1, d)
    return (y_prompt, y_sample) + tuple(o[None] for o in (bp, bs, cp, cs, np_, ns, mp, ms))
```

```python
import functools

import jax
import jax.numpy as jnp
from jax import lax
from jax.experimental import pallas as pl
from jax.experimental.pallas import tpu as pltpu

F32 = jnp.float32
BF16 = jnp.bfloat16
I32 = jnp.int32

EPS = 1e-6
NEG_INIT = -1e30
PEER_TOPK = 16
LANES = 128
VMEM_LIMIT = 56 * 1024 * 1024


def _cparams(*sem):
    return pltpu.CompilerParams(dimension_semantics=sem, vmem_limit_bytes=VMEM_LIMIT)


def _tile(target, *extents):
    t = min([target] + [e for e in extents if e])
    while any(e % t for e in extents):
        t -= LANES
    return t


def _sigmoid(x):
    return 1.0 / (1.0 + jnp.exp(-x))


def _log_sigmoid(x):
    return jnp.minimum(x, 0.0) - jnp.log1p(jnp.exp(-jnp.abs(x)))


def _split3(x):
    hi = x.astype(BF16)
    r1 = x - hi.astype(F32)
    mid = r1.astype(BF16)
    lo = (r1 - mid.astype(F32)).astype(BF16)
    return hi, mid, lo


def _ada_kernel(c_ref, w_ref, b_ref, o_ref):
    c = c_ref[...]
    a = (c * _sigmoid(c)).astype(BF16)
    o_ref[...] = jnp.dot(a, w_ref[...].astype(BF16), preferred_element_type=F32) + b_ref[...]


def _ada(c, w, b):
    m, d = c.shape
    n = w.shape[1]
    tn = _tile(1024, n)
    return pl.pallas_call(
        _ada_kernel,
        grid=(n // tn,),
        in_specs=[pl.BlockSpec((m, d), lambda j: (0, 0)),
                  pl.BlockSpec((d, tn), lambda j: (0, j)),
                  pl.BlockSpec((1, tn), lambda j: (0, j))],
        out_specs=pl.BlockSpec((m, tn), lambda j: (0, j)),
        out_shape=jax.ShapeDtypeStruct((m, n), F32),
        compiler_params=_cparams("arbitrary"),
        name="ada_mod",
    )(c, w, b.reshape(1, n))


def _norm_mod_kernel(x_ref, g_ref, sc_ref, sh_ref, o_ref):
    x = x_ref[...]
    y = x * lax.rsqrt(jnp.mean(x * x, axis=-1, keepdims=True) + EPS)
    o_ref[...] = (y * g_ref[...] * (1.0 + sc_ref[0]) + sh_ref[0]).astype(o_ref.dtype)


def _norm_mod(x, g, mod3, sh_idx, sc_idx, grp_rows, tm):
    t, d = x.shape
    r = mod3.shape[1]
    per = grp_rows // tm
    return pl.pallas_call(
        _norm_mod_kernel,
        grid=(t // tm,),
        in_specs=[pl.BlockSpec((tm, d), lambda i: (i, 0)),
                  pl.BlockSpec((1, d), lambda i: (0, 0)),
                  pl.BlockSpec((1, r, d), lambda i: (i // per, 0, sc_idx)),
                  pl.BlockSpec((1, r, d), lambda i: (i // per, 0, sh_idx))],
        out_specs=pl.BlockSpec((tm, d), lambda i: (i, 0)),
        out_shape=jax.ShapeDtypeStruct((t, d), BF16),
        compiler_params=_cparams("arbitrary"),
        name="norm_mod",
    )(x, g.reshape(1, d), mod3, mod3)


def _load_weight_tile(w_ref, transposed):
    w = w_ref[...]
    return (w.T if transposed else w).astype(BF16)


def _mm_kernel(*refs, act, has_mul, has_add, has_res, w_transposed):
    a_ref, w_ref = refs[0], refs[1]
    o_ref, wb_ref = refs[-2], refs[-1]
    extra = list(refs[2:-2])

    @pl.when(pl.program_id(1) == 0)
    def _():
        wb_ref[...] = _load_weight_tile(w_ref, w_transposed)

    acc = jnp.dot(a_ref[...], wb_ref[...], preferred_element_type=F32)
    if act == "sigmoid":
        acc = _sigmoid(acc)
    if has_mul:
        acc = acc * extra.pop(0)[...]
    if has_add:
        acc = acc + extra.pop(0)[...]
    if has_res:
        x_ref, g_ref = extra
        acc = x_ref[...] + g_ref[0] * acc
    o_ref[...] = acc.astype(o_ref.dtype)


def _mm(a, w, *, col0, ncols, tm, tn, out_dtype, act=None, mul=None, add=None, res=None, w_transposed=False,
        name="mm"):
    m, k = a.shape
    tm = min(tm, m)
    tn = _tile(tn, ncols, col0, mul[1] if mul else 0)
    cb = col0 // tn
    nj = ncols // tn
    w_spec = (pl.BlockSpec((tn, k), lambda j, i: (cb + j, 0)) if w_transposed
              else pl.BlockSpec((k, tn), lambda j, i: (0, cb + j)))
    in_specs = [pl.BlockSpec((tm, k), lambda j, i: (i, 0)), w_spec]
    operands = [a, w]
    tile = pl.BlockSpec((tm, tn), lambda j, i: (i, j))
    if mul is not None:
        mb = mul[1] // tn
        in_specs.append(pl.BlockSpec((tm, tn), lambda j, i: (i, mb + j)))
        operands.append(mul[0])
    if add is not None:
        in_specs.append(tile)
        operands.append(add)
    if res is not None:
        x, mod3, g_idx, grp_rows = res
        per = grp_rows // tm
        in_specs += [tile, pl.BlockSpec((1, mod3.shape[1], tn), lambda j, i: (i // per, 0, g_idx * nj + j))]
        operands += [x, mod3]
    return pl.pallas_call(
        functools.partial(_mm_kernel, act=act, has_mul=mul is not None, has_add=add is not None,
                          has_res=res is not None, w_transposed=w_transposed),
        grid=(nj, m // tm),
        in_specs=in_specs,
        out_specs=tile,
        out_shape=jax.ShapeDtypeStruct((m, ncols), out_dtype),
        scratch_shapes=[pltpu.VMEM((k, tn), BF16)],
        compiler_params=_cparams("arbitrary", "arbitrary"),
        name=name,
    )(*operands)


def _mm_glu_kernel(a_ref, wa_ref, wg_ref, o_ref, wab_ref, wgb_ref):
    @pl.when(pl.program_id(1) == 0)
    def _():
        wab_ref[...] = _load_weight_tile(wa_ref, True)
        wgb_ref[...] = _load_weight_tile(wg_ref, True)

    a = a_ref[...]
    lin = jnp.dot(a, wab_ref[...], preferred_element_type=F32)
    gate = jnp.dot(a, wgb_ref[...], preferred_element_type=F32)
    o_ref[...] = lin * _sigmoid(gate)


def _mm_glu(a, w_t, *, d_half, tm, tn):
    m, k = a.shape
    tm = min(tm, m)
    tn = _tile(tn, d_half)
    off = d_half // tn
    return pl.pallas_call(
        _mm_glu_kernel,
        grid=(d_half // tn, m // tm),
        in_specs=[pl.BlockSpec((tm, k), lambda j, i: (i, 0)),
                  pl.BlockSpec((tn, k), lambda j, i: (j, 0)),
                  pl.BlockSpec((tn, k), lambda j, i: (off + j, 0))],
        out_specs=pl.BlockSpec((tm, tn), lambda j, i: (i, j)),
        out_shape=jax.ShapeDtypeStruct((m, d_half), F32),
        scratch_shapes=[pltpu.VMEM((k, tn), BF16), pltpu.VMEM((k, tn), BF16)],
        compiler_params=_cparams("arbitrary", "arbitrary"),
        name="mm_glu",
    )(a, w_t, w_t)


_HALO = 32
_CONV_RC = 64


def _ln_silu(y, lg, lb):
    mu = jnp.mean(y, axis=-1, keepdims=True)
    yc = y - mu
    var = jnp.mean(yc * yc, axis=-1, keepdims=True)
    z = yc * lax.rsqrt(var + EPS) * lg + lb
    return z * _sigmoid(z)


def _conv_prompt_kernel(u_ref, dw_ref, db_ref, lg_ref, lb_ref, o_ref, win_ref, y_ref, sh_ref, *, tt, kw):
    d = u_ref.shape[2]
    t = pl.program_id(1)

    @pl.when(t == 0)
    def _():
        win_ref[0:_HALO, :] = jnp.zeros((_HALO, d), F32)

    @pl.when(t > 0)
    def _():
        win_ref[0:_HALO, :] = win_ref[tt:tt + _HALO, :]

    win_ref[_HALO:_HALO + tt, :] = u_ref[0]
    base = _HALO - (kw - 1)
    rc = min(_CONV_RC, tt)
    for mis in range(1, 8):
        sh_ref[mis - 1] = win_ref[mis:mis + tt + _HALO - 8, :]

    def row_chunk(r, carry):
        r0 = pl.multiple_of(r * rc, rc)
        for c in range(d // LANES):
            ls = slice(c * LANES, (c + 1) * LANES)
            acc = jnp.broadcast_to(db_ref[:, ls], (rc, LANES))
            for mis in range(8):
                taps = [j for j in range(kw) if (base + j) % 8 == mis]
                span = max(base + j - mis for j in taps) + rc
                w = win_ref[pl.ds(r0, span), ls] if mis == 0 else sh_ref[mis - 1, pl.ds(r0, span), ls]
                for j in taps:
                    off = base + j - mis
                    acc = acc + w[off:off + rc, :] * dw_ref[j:j + 1, ls]
            y_ref[pl.ds(r0, rc), ls] = acc
        return carry

    lax.fori_loop(0, tt // rc, row_chunk, 0)
    o_ref[0] = _ln_silu(y_ref[...], lg_ref[...], lb_ref[...]).astype(o_ref.dtype)


def _conv_prompt(u3, dw, db, lg, lb, tt):
    b, t, d = u3.shape
    tt = min(tt, t)
    kw = dw.shape[0]
    assert kw - 1 <= _HALO <= tt
    return pl.pallas_call(
        functools.partial(_conv_prompt_kernel, tt=tt, kw=kw),
        grid=(b, t // tt),
        in_specs=[pl.BlockSpec((1, tt, d), lambda i, j: (i, j, 0)),
                  pl.BlockSpec((kw, d), lambda i, j: (0, 0)),
                  pl.BlockSpec((1, d), lambda i, j: (0, 0)),
                  pl.BlockSpec((1, d), lambda i, j: (0, 0)),
                  pl.BlockSpec((1, d), lambda i, j: (0, 0))],
        out_specs=pl.BlockSpec((1, tt, d), lambda i, j: (i, j, 0)),
        out_shape=jax.ShapeDtypeStruct((b, t, d), BF16),
        scratch_shapes=[pltpu.VMEM((_HALO + tt, d), F32), pltpu.VMEM((tt, d), F32),
                        pltpu.VMEM((7, tt + _HALO - 8, d), F32)],
        compiler_params=_cparams("arbitrary", "arbitrary"),
        name="conv_prompt",
    )(u3, dw, db.reshape(1, d), lg.reshape(1, d), lb.reshape(1, d))


def _conv_sample_kernel(buf_ref, u_ref, dw_ref, db_ref, lg_ref, lb_ref, o_ref, nb_ref, *, kw):
    u = u_ref[...]
    acc = u * dw_ref[kw - 1:kw, :] + db_ref[...]
    for j in range(kw - 1):
        acc = acc + buf_ref[j] * dw_ref[j:j + 1, :]
    o_ref[...] = _ln_silu(acc, lg_ref[...], lb_ref[...]).astype(o_ref.dtype)
    for j in range(kw - 2):
        nb_ref[j] = buf_ref[j + 1]
    nb_ref[kw - 2] = u


def _conv_sample(buf_t, u, dw, db, lg, lb, bb=16):
    b, d = u.shape
    kw = dw.shape[0]
    bb = min(bb, b)
    return pl.pallas_call(
        functools.partial(_conv_sample_kernel, kw=kw),
        grid=(b // bb,),
        in_specs=[pl.BlockSpec((kw - 1, bb, d), lambda i: (0, i, 0)),
                  pl.BlockSpec((bb, d), lambda i: (i, 0)),
                  pl.BlockSpec((kw, d), lambda i: (0, 0)),
                  pl.BlockSpec((1, d), lambda i: (0, 0)),
                  pl.BlockSpec((1, d), lambda i: (0, 0)),
                  pl.BlockSpec((1, d), lambda i: (0, 0))],
        out_specs=[pl.BlockSpec((bb, d), lambda i: (i, 0)),
                   pl.BlockSpec((kw - 1, bb, d), lambda i: (0, i, 0))],
        out_shape=[jax.ShapeDtypeStruct((b, d), BF16),
                   jax.ShapeDtypeStruct((kw - 1, b, d), F32)],
        compiler_params=_cparams("arbitrary"),
        name="conv_sample",
    )(buf_t, u, dw, db.reshape(1, d), lg.reshape(1, d), lb.reshape(1, d))


_MLSTM_CB = 256


def _mlstm_prompt_kernel(bias_ref, q_ref, k_ref, v_ref, ifc_ref, ifr_ref, og_ref, ng_ref,
                         h_ref, c_ref, n_ref, m_ref, cb_ref, ns_ref, ms_ref, *, n_heads):
    hd = pl.program_id(1)
    ck = pl.program_id(2)
    L, dk = q_ref.shape
    dv = v_ref.shape[1]
    scale = dk ** -0.5

    @pl.when(ck == 0)
    def _():
        c_ref[...] = jnp.zeros(c_ref.shape, F32)
        cb_ref[...] = jnp.zeros(cb_ref.shape, BF16)
        ns_ref[...] = jnp.zeros(ns_ref.shape, F32)
        ms_ref[...] = jnp.full(ms_ref.shape, NEG_INIT, F32)

    q = q_ref[...]
    k = k_ref[...]
    v = v_ref[...]
    bi = bias_ref[hd]
    bf = bias_ref[n_heads + hd]
    col = ifc_ref[0, 0]
    li_c = col[:, 0:1] + bi
    lf_c = _log_sigmoid(col[:, 1:2] + bf)
    row = ifr_ref[0, 0]
    li_r = row[0:1, :] + bi
    lf_r = _log_sigmoid(row[1:2, :] + bf)

    r_io = lax.broadcasted_iota(I32, (L, L), 0)
    c_io = lax.broadcasted_iota(I32, (L, L), 1)
    causal = r_io >= c_io
    tri = jnp.where(causal, 1.0, 0.0).astype(BF16)
    tri_t = jnp.where(r_io <= c_io, 1.0, 0.0).astype(BF16)
    b_c = sum(jnp.dot(tri, p, preferred_element_type=F32)
              for p in _split3(jnp.broadcast_to(lf_c, (L, LANES))))[:, 0:1]
    b_r = sum(jnp.dot(p, tri_t, preferred_element_type=F32)
              for p in _split3(jnp.broadcast_to(lf_r, (16, L))))[0:1, :]

    m_prev = ms_ref[0:1, 0:1]
    b_last = b_c[L - 1:L, :]
    dmat = jnp.where(causal, b_c - (b_r - li_r), -jnp.inf)
    inter = b_c + m_prev
    m_t = jnp.maximum(inter, jnp.max(dmat, axis=-1, keepdims=True))
    w_inter = jnp.exp(inter - m_t)
    s = lax.dot_general(q, k, (((1,), (1,)), ((), ())), preferred_element_type=F32)
    s = s * scale * jnp.exp(dmat - m_t)
    qc = jnp.dot(q, cb_ref[...], preferred_element_type=F32)
    num = w_inter * qc + jnp.dot(s.astype(BF16), v, preferred_element_type=F32)
    qn = jnp.sum(q.astype(F32) * ns_ref[...], axis=-1, keepdims=True)
    den = w_inter * qn + jnp.sum(s, axis=-1, keepdims=True)
    h = num / jnp.maximum(jnp.abs(den), jnp.exp(-m_t))
    hn = h * lax.rsqrt(jnp.mean(h * h, axis=-1, keepdims=True) + EPS)
    h_ref[...] = (hn * ng_ref[...] * og_ref[...]).astype(h_ref.dtype)

    g_c = b_last - b_c + li_c
    m_new = jnp.maximum(b_last + m_prev, jnp.max(g_c, axis=0, keepdims=True))
    w_s = jnp.exp(g_c - m_new)
    decay = jnp.exp(b_last + m_prev - m_new)
    kw = k.astype(F32) * (w_s * scale)
    kwb = kw.astype(BF16)
    cbw = min(_MLSTM_CB, dv)
    for j in range(dv // cbw):
        cs = slice(j * cbw, (j + 1) * cbw)
        kv = lax.dot_general(kwb, v[:, cs], (((0,), (0,)), ((), ())), preferred_element_type=F32)
        c_new = decay * c_ref[0, 0, :, cs] + kv
        c_ref[0, 0, :, cs] = c_new
        cb_ref[:, cs] = c_new.astype(BF16)
    n_new = decay * ns_ref[...] + jnp.sum(kw, axis=0, keepdims=True)
    ns_ref[...] = n_new
    ms_ref[...] = jnp.broadcast_to(m_new, ms_ref.shape)
    n_ref[0, 0] = n_new
    m_ref[0, 0] = jnp.broadcast_to(m_new, (1, LANES))


def _mlstm_prompt(qkv, og, if_pre, if_b, norm_g, batch, seq, n_heads, chunk):
    dh = qkv.shape[1] // (3 * n_heads)
    chunk = min(chunk, seq)
    nc = seq // chunk
    ifx = if_pre.reshape(batch, seq, 2, n_heads)
    ifc = ifx.transpose(0, 3, 1, 2)
    ifr = ifx.transpose(0, 3, 2, 1)
    row = lambda b, h, c: (b * nc + c)
    h_out, c_out, n_out, m_out = pl.pallas_call(
        functools.partial(_mlstm_prompt_kernel, n_heads=n_heads),
        grid=(batch, n_heads, nc),
        in_specs=[pl.BlockSpec(memory_space=pltpu.SMEM),
                  pl.BlockSpec((chunk, dh), lambda b, h, c: (row(b, h, c), h)),
                  pl.BlockSpec((chunk, dh), lambda b, h, c: (row(b, h, c), n_heads + h)),
                  pl.BlockSpec((chunk, dh), lambda b, h, c: (row(b, h, c), 2 * n_heads + h)),
                  pl.BlockSpec((1, 1, chunk, 2), lambda b, h, c: (b, h, c, 0)),
                  pl.BlockSpec((1, 1, 2, chunk), lambda b, h, c: (b, h, 0, c)),
                  pl.BlockSpec((chunk, dh), lambda b, h, c: (row(b, h, c), h)),
                  pl.BlockSpec((1, dh), lambda b, h, c: (0, h))],
        out_specs=[pl.BlockSpec((chunk, dh), lambda b, h, c: (row(b, h, c), h)),
                   pl.BlockSpec((1, 1, dh, dh), lambda b, h, c: (b, h, 0, 0)),
                   pl.BlockSpec((1, 1, 1, dh), lambda b, h, c: (b, h, 0, 0)),
                   pl.BlockSpec((1, 1, 1, LANES), lambda b, h, c: (b, h, 0, 0))],
        out_shape=[jax.ShapeDtypeStruct((batch * seq, n_heads * dh), BF16),
                   jax.ShapeDtypeStruct((batch, n_heads, dh, dh), F32),
                   jax.ShapeDtypeStruct((batch, n_heads, 1, dh), F32),
                   jax.ShapeDtypeStruct((batch, n_heads, 1, LANES), F32)],
        scratch_shapes=[pltpu.VMEM((dh, dh), BF16), pltpu.VMEM((1, dh), F32), pltpu.VMEM((8, LANES), F32)],
        compiler_params=_cparams("arbitrary", "arbitrary", "arbitrary"),
        name="mlstm_prompt",
    )(if_b, qkv, qkv, qkv, ifc, ifr, og, norm_g.reshape(1, -1))
    return h_out, c_out, n_out[:, :, 0, :], m_out[:, :, 0, 0]


def _mlstm_sample_head_setup(b, q_ref, k_ref, qt_ref, kt_ref):
    @pl.when(b == 0)
    def _():
        for src, dst in ((q_ref, qt_ref), (k_ref, kt_ref)):
            for i, p in enumerate(_split3(src[...].T)):
                dst[i] = p


def _mlstm_sample_columns(b, qt_ref, kt_ref):
    nb = qt_ref.shape[2]
    sel = jnp.where(lax.broadcasted_iota(I32, (nb, LANES), 0) == b, 1.0, 0.0).astype(BF16)
    return tuple(sum(jnp.dot(t_ref[i], sel, preferred_element_type=F32) for i in range(3))
                 for t_ref in (qt_ref, kt_ref))


def _mlstm_sample_item(hd, b, columns, bias_ref, ifp_ref, mst_ref, q_ref, k_ref, v_ref, c_ref, n_ref, og_ref,
                       ng_ref, h_ref, co_ref, no_ref, mo_ref, qt_ref, kt_ref, *, n_heads):
    nb, dk = q_ref.shape
    dv = v_ref.shape[1]
    scale = dk ** -0.5
    qb, kb = columns
    li = jnp.full((1, 1), ifp_ref[b, hd] + bias_ref[hd], F32)
    lf = _log_sigmoid(jnp.full((1, 1), ifp_ref[b, n_heads + hd] + bias_ref[n_heads + hd], F32))
    m_prev = jnp.full((1, 1), mst_ref[b, hd], F32)
    m_t = jnp.maximum(lf + m_prev, li)
    decay = jnp.exp(lf + m_prev - m_t)
    w_s = jnp.exp(li - m_t)

    q_row = q_ref[pl.ds(b, 1), :]
    k_row = k_ref[pl.ds(b, 1), :]
    v_row = v_ref[pl.ds(b, 1), :]
    n_row = n_ref[0, 0]
    sval = jnp.sum(q_row * k_row, axis=-1, keepdims=True) * scale * w_s
    den = decay * jnp.sum(q_row * n_row, axis=-1, keepdims=True) + sval
    denom = jnp.maximum(jnp.abs(den), jnp.exp(-m_t))
    wk = w_s * scale
    pieces = []
    for j in range(dv // LANES):
        ls = slice(j * LANES, (j + 1) * LANES)
        cj = c_ref[0, 0, :, ls]
        vj = v_row[:, ls]
        qc = jnp.sum(qb * cj, axis=0, keepdims=True)
        co_ref[0, 0, :, ls] = decay * cj + kb * (wk * vj)
        pieces.append((decay * qc + sval * vj) / denom)
    h = jnp.concatenate(pieces, axis=1)
    hn = h * lax.rsqrt(jnp.mean(h * h, axis=-1, keepdims=True) + EPS)
    h_ref[0] = (hn * ng_ref[...] * og_ref[0]).astype(h_ref.dtype)
    no_ref[0, 0] = decay * n_row + wk * k_row
    mo_ref[0, 0] = jnp.broadcast_to(m_t, (1, LANES))


def _mlstm_sample_kernel(*refs, n_heads):
    b = pl.program_id(1)
    _mlstm_sample_head_setup(b, refs[3], refs[4], refs[-2], refs[-1])
    _mlstm_sample_item(pl.program_id(0), b, _mlstm_sample_columns(b, refs[-2], refs[-1]), *refs, n_heads=n_heads)


def _mlstm_sample_plumbing(qkv, og, if_pre, if_b, norm_g, st_c, st_n, st_m, n_heads, item):
    nb = qkv.shape[0]
    dh = qkv.shape[1] // (3 * n_heads)

    def at(fn):
        return lambda *ids: fn(*item(*ids))

    smem = pl.BlockSpec(memory_space=pltpu.SMEM)
    in_specs = [smem, smem, smem,
                pl.BlockSpec((nb, dh), at(lambda h, b: (0, h))),
                pl.BlockSpec((nb, dh), at(lambda h, b: (0, n_heads + h))),
                pl.BlockSpec((nb, dh), at(lambda h, b: (0, 2 * n_heads + h))),
                pl.BlockSpec((1, 1, dh, dh), at(lambda h, b: (b, h, 0, 0))),
                pl.BlockSpec((1, 1, 1, dh), at(lambda h, b: (b, h, 0, 0))),
                pl.BlockSpec((1, 1, dh), at(lambda h, b: (b, 0, h))),
                pl.BlockSpec((1, dh), at(lambda h, b: (0, h)))]
    operands = [if_b, if_pre, st_m, qkv, qkv, qkv, st_c, st_n.reshape(nb, n_heads, 1, dh),
                og.reshape(nb, 1, n_heads * dh), norm_g.reshape(1, -1)]
    out_specs = [pl.BlockSpec((1, 1, dh), at(lambda h, b: (b, 0, h))),
                 pl.BlockSpec((1, 1, dh, dh), at(lambda h, b: (b, h, 0, 0))),
                 pl.BlockSpec((1, 1, 1, dh), at(lambda h, b: (b, h, 0, 0))),
                 pl.BlockSpec((1, 1, 1, LANES), at(lambda h, b: (b, h, 0, 0)))]
    out_shape = [jax.ShapeDtypeStruct((nb, 1, n_heads * dh), BF16),
                 jax.ShapeDtypeStruct((nb, n_heads, dh, dh), F32),
                 jax.ShapeDtypeStruct((nb, n_heads, 1, dh), F32),
                 jax.ShapeDtypeStruct((nb, n_heads, 1, LANES), F32)]
    scratch = [pltpu.VMEM((3, dh, nb), BF16), pltpu.VMEM((3, dh, nb), BF16)]
    return operands, in_specs, out_specs, out_shape, scratch


def _mlstm_sample_results(h_out, c_out, n_out, m_out):
    return h_out.reshape(h_out.shape[0], -1), c_out, n_out[:, :, 0, :], m_out[:, :, 0, 0]


def _mlstm_sample(qkv, og, if_pre, if_b, norm_g, st_c, st_n, st_m, n_heads):
    operands, in_specs, out_specs, out_shape, scratch = _mlstm_sample_plumbing(
        qkv, og, if_pre, if_b, norm_g, st_c, st_n, st_m, n_heads, lambda h, b: (h, b))
    outs = pl.pallas_call(
        functools.partial(_mlstm_sample_kernel, n_heads=n_heads),
        grid=(n_heads, qkv.shape[0]),
        in_specs=in_specs, out_specs=out_specs, out_shape=out_shape, scratch_shapes=scratch,
        compiler_params=_cparams("arbitrary", "arbitrary"),
        name="mlstm_sample",
    )(*operands)
    return _mlstm_sample_results(*outs)


def _topk_rows(problems, n_take):
    state = []
    for s, payload in problems:
        iota = lax.broadcasted_iota(I32, s.shape, 0).astype(F32)
        state.append([s, iota, payload, [], []])
    for _ in range(n_take):
        for st in state:
            s, iota, payload, vals, picks = st
            m = jnp.max(s, axis=0, keepdims=True)
            pos = jnp.min(jnp.where(s == m, iota, float(s.shape[0])), axis=0, keepdims=True)
            hit = iota == pos
            vals.append(m)
            picks.append(pos if payload is None
                         else jnp.max(jnp.where(hit, payload, -1.0), axis=0, keepdims=True))
            st[0] = jnp.where(hit, -jnp.inf, s)
    return [(st[3], st[4]) for st in state]


def _peer_pairs(topk):
    return [(i, j) for i in range(topk) for j in range(topk) if (i + 1) * (j + 1) <= topk]


def _peer_topk_kernel(q_ref, sk_ref, a_ref, b_ref, g_ref, cv_ref, ce_ref, *, topk):
    tb = q_ref.shape[0]
    nk, dsub = sk_ref.shape[2], sk_ref.shape[3]
    pairs = _peer_pairs(topk)
    sk = [sk_ref[0, p].astype(BF16) for p in range(2)]

    n_groups = tb // LANES
    per_iter = cv_ref.shape[0]

    def groups(gi, carry):
        starts = [pl.multiple_of((gi * per_iter + n) * LANES, LANES) for n in range(per_iter)]
        for n, t0 in enumerate(starts):
            scores = []
            for p in range(2):
                qp = q_ref[pl.ds(t0, LANES), p * dsub:(p + 1) * dsub].astype(BF16)
                scores.append(lax.dot_general(sk[p], qp, (((1,), (1,)), ((), ())),
                                              preferred_element_type=F32))
            (sv0, si0), (sv1, si1) = _topk_rows([(scores[0], None), (scores[1], None)], topk)
            cv_ref[n] = jnp.full(cv_ref.shape[1:], -jnp.inf, F32)
            ce_ref[n] = jnp.zeros(ce_ref.shape[1:], F32)
            for r, (i, j) in enumerate(pairs):
                cv_ref[n, r:r + 1, :] = sv0[i] + sv1[j]
                ce_ref[n, r:r + 1, :] = si0[i] * float(nk) + si1[j]
        picked = _topk_rows([(cv_ref[n], ce_ref[n]) for n in range(per_iter)], topk)
        for t0, (tv, te) in zip(starts, picked):
            ex = [jnp.exp(x - tv[0]) for x in tv]
            tot = sum(ex)
            for r in range(topk):
                first = jnp.floor(te[r] * (1.0 / nk))
                a_ref[r:r + 1, pl.ds(t0, LANES)] = first.astype(I32)
                b_ref[r:r + 1, pl.ds(t0, LANES)] = (te[r] - first * float(nk)).astype(I32)
                g_ref[r:r + 1, pl.ds(t0, LANES)] = ex[r] / tot
        return carry

    lax.fori_loop(0, n_groups // per_iter, groups, 0)


def _peer_topk(q, subkeys, topk, tb):
    t = q.shape[0]
    nh, _, nk, dsub = subkeys.shape
    tb = min(tb, t)
    assert nk & (nk - 1) == 0 and nk * nk < 2 ** 24
    n_pad = -(-len(_peer_pairs(topk)) // 8) * 8
    per_iter = 2 if (tb // LANES) % 2 == 0 else 1
    out = jax.ShapeDtypeStruct((nh * topk, t), I32)
    return pl.pallas_call(
        functools.partial(_peer_topk_kernel, topk=topk),
        grid=(t // tb, nh),
        in_specs=[pl.BlockSpec((tb, 2 * dsub), lambda i, h: (i, h)),
                  pl.BlockSpec((1, 2, nk, dsub), lambda i, h: (h, 0, 0, 0))],
        out_specs=[pl.BlockSpec((topk, tb), lambda i, h: (h, i))] * 3,
        out_shape=[out, out, jax.ShapeDtypeStruct((nh * topk, t), F32)],
        scratch_shapes=[pltpu.VMEM((per_iter, n_pad, LANES), F32)] * 2,
        compiler_params=_cparams("arbitrary", "arbitrary"),
        name="peer_topk",
    )(q, subkeys)


def _peer_gates_kernel(a_ref, b_ref, g_ref, o_ref, at_ref, bt_ref, gt_ref):
    nk = o_ref.shape[1]
    tg = o_ref.shape[0]
    at_ref[...] = a_ref[...].T
    bt_ref[...] = b_ref[...].T
    gt_ref[...] = g_ref[...].T
    key = lax.broadcasted_iota(I32, (nk, at_ref.shape[1]), 0)

    def token(t, carry):
        a = at_ref[pl.ds(t, 1), :]
        b = bt_ref[pl.ds(t, 1), :]
        g = gt_ref[pl.ds(t, 1), :]
        first = jnp.where(key == a, g, 0.0).astype(BF16)
        second = jnp.where(key == b, 1.0, 0.0).astype(BF16)
        o_ref[t] = lax.dot_general(first, second, (((1,), (1,)), ((), ())), preferred_element_type=F32)
        return carry

    lax.fori_loop(0, tg, token, 0, unroll=16)


def _peer_gates(a_idx, b_idx, gates, nk, tg):
    hk, t = a_idx.shape
    tg = min(tg, t)
    return pl.pallas_call(
        _peer_gates_kernel,
        grid=(t // tg,),
        in_specs=[pl.BlockSpec((hk, tg), lambda i: (0, i))] * 3,
        out_specs=pl.BlockSpec((tg, nk, nk), lambda i: (i, 0, 0)),
        out_shape=jax.ShapeDtypeStruct((t, nk, nk), F32),
        scratch_shapes=[pltpu.VMEM((tg, hk), I32), pltpu.VMEM((tg, hk), I32), pltpu.VMEM((tg, hk), F32)],
        compiler_params=_cparams("arbitrary"),
        name="peer_gates",
    )(a_idx, b_idx, gates)


_SQRT_HALF = 0.7071067811865476


_G_KEYS = 8
_N_RIDER_IN = 10
_N_RIDER_OUT = 4


def _peer_dense_kernel(*refs, n_blocks, keys_per_step, rider):
    x_ref, u_ref, v_ref, g_ref = refs[:4]
    if rider is None:
        o_ref, coef_ref = refs[4:]
    else:
        rider_in = refs[4:4 + _N_RIDER_IN]
        o_ref = refs[4 + _N_RIDER_IN]
        rider_out = refs[5 + _N_RIDER_IN:5 + _N_RIDER_IN + _N_RIDER_OUT]
        coef_ref = refs[5 + _N_RIDER_IN + _N_RIDER_OUT]
        rider_scratch = refs[6 + _N_RIDER_IN + _N_RIDER_OUT:]
    e = pl.program_id(1)
    cur = e % 2
    prev = 1 - cur

    @pl.when(e == 0)
    def _():
        o_ref[...] = jnp.zeros(o_ref.shape, F32)
        coef_ref[1] = jnp.zeros(coef_ref.shape[1:], BF16)

    if rider is not None:
        n_heads, n_items, nb = rider
        item = jnp.minimum(pl.program_id(0) * (n_blocks + 1) + e, n_items - 1)
        _mlstm_sample_head_setup(item % nb, rider_in[3], rider_in[4], *rider_scratch)
        columns = _mlstm_sample_columns(item % nb, *rider_scratch)

    a = lax.dot_general(x_ref[...], u_ref[...], (((1,), (1,)), ((), ())), preferred_element_type=F32)
    o_ref[...] += jnp.dot(coef_ref[prev], v_ref[...], preferred_element_type=F32)
    if rider is not None:
        _mlstm_sample_item(item // nb, item % nb, columns, *rider_in, *rider_out, *rider_scratch, n_heads=n_heads)
    act = 0.5 * a * (1.0 + lax.erf(a * _SQRT_HALF))
    per_block = g_ref.shape[1] // keys_per_step
    first = 0 if per_block == 1 else (jnp.minimum(e, n_blocks - 1) % per_block) * keys_per_step
    g = jnp.concatenate([g_ref[:, first + s, :] for s in range(keys_per_step)], axis=1)
    coef_ref[cur] = (g * act).astype(BF16)


def _peer_dense(x, u_tab, v_tab, gmat, tb, keys_per_step, rider_args=None):
    t, d = x.shape
    nk = gmat.shape[1]
    tb = min(tb, t)
    eb = keys_per_step * nk
    ne = nk // keys_per_step
    per_block = _G_KEYS // keys_per_step
    cur_blk = lambda e: jnp.minimum(e, ne - 1)
    once = pl.Buffered(1)
    in_specs = [pl.BlockSpec((tb, d), lambda i, e: (i, 0), pipeline_mode=once),
                pl.BlockSpec((eb, d), lambda i, e: (cur_blk(e), 0)),
                pl.BlockSpec((eb, d), lambda i, e: (jnp.maximum(e - 1, 0), 0)),
                pl.BlockSpec((tb, _G_KEYS, nk), lambda i, e: (i, cur_blk(e) // per_block, 0))]
    operands = [x, u_tab, v_tab, gmat]
    out_specs = [pl.BlockSpec((tb, d), lambda i, e: (i, 0), pipeline_mode=once)]
    out_shape = [jax.ShapeDtypeStruct((t, d), F32)]
    scratch = [pltpu.VMEM((2, tb, eb), BF16)]
    rider = None
    if rider_args is not None:
        n_heads = rider_args[-1]
        nb = rider_args[0].shape[0]
        n_items = n_heads * nb
        assert (t // tb) * (ne + 1) >= n_items

        def item(i, e):
            k = jnp.minimum(i * (ne + 1) + e, n_items - 1)
            return k // nb, k % nb

        r_ops, r_in, r_out, r_shape, r_scratch = _mlstm_sample_plumbing(*rider_args, item)
        assert len(r_ops) == _N_RIDER_IN and len(r_out) == _N_RIDER_OUT
        operands += r_ops
        in_specs += r_in
        out_specs += r_out
        out_shape += r_shape
        scratch += r_scratch
        rider = (n_heads, n_items, nb)
    outs = pl.pallas_call(
        functools.partial(_peer_dense_kernel, n_blocks=ne, keys_per_step=keys_per_step, rider=rider),
        grid=(t // tb, ne + 1),
        in_specs=in_specs, out_specs=out_specs, out_shape=out_shape, scratch_shapes=scratch,
        compiler_params=_cparams("arbitrary", "arbitrary"),
        name="peer_dense" if rider is None else "peer_dense_mlstm_sample",
    )(*operands)
    return outs[0], (None if rider is None else _mlstm_sample_results(*outs[1:]))


def _final_kernel(x_ref, p_ref, g_ref, fg_ref, o_ref):
    x = x_ref[...] + g_ref[0] * p_ref[...]
    o_ref[...] = x * lax.rsqrt(jnp.mean(x * x, axis=-1, keepdims=True) + EPS) * fg_ref[...]


def _final(x, p, mod3, g_idx, final_g, grp_rows, tm):
    t, d = x.shape
    r = mod3.shape[1]
    per = grp_rows // tm
    return pl.pallas_call(
        _final_kernel,
        grid=(t // tm,),
        in_specs=[pl.BlockSpec((tm, d), lambda i: (i, 0)),
                  pl.BlockSpec((tm, d), lambda i: (i, 0)),
                  pl.BlockSpec((1, r, d), lambda i: (i // per, 0, g_idx)),
                  pl.BlockSpec((1, d), lambda i: (0, 0))],
        out_specs=pl.BlockSpec((tm, d), lambda i: (i, 0)),
        out_shape=jax.ShapeDtypeStruct((t, d), F32),
        compiler_params=_cparams("arbitrary"),
        name="final_norm",
    )(x, p, mod3, final_g.reshape(1, d))


_MM_ROWS = 1024
_NORM_ROWS = 512
_PEER_TB = 1024
_PEER_KEYS_PER_STEP = 2


def _in_proj(x, mod3, grp_rows, w, n_heads, qkv_dtype):
    t, d = x.shape
    d_ml = w["mlstm_norm_g"].shape[0]
    tm = min(_MM_ROWS, grp_rows, t)
    h1 = _norm_mod(x, w["norm1_g"], mod3, 0, 1, grp_rows, min(_NORM_ROWS, tm))
    u = _mm_glu(h1, w["w_in_t"], d_half=d, tm=tm, tn=512)
    proj = functools.partial(_mm, h1, tm=tm, w_transposed=True)
    qkv = proj(w["w_in_t"], col0=2 * d, ncols=3 * d_ml, tn=1024, out_dtype=qkv_dtype, name="mm_qkv")
    og = proj(w["w_in_t"], col0=2 * d + 3 * d_ml, ncols=d_ml, tn=1024, out_dtype=F32, act="sigmoid",
              name="mm_ogate")
    if_pre = proj(w["w_if_t"], col0=0, ncols=LANES, tn=LANES, out_dtype=F32, name="mm_if")[:, :2 * n_heads]
    gates = proj(w["w_gate_t"], col0=0, ncols=2 * d, tn=1024, out_dtype=F32, act="sigmoid", name="mm_gates")
    return u, qkv, og, if_pre, gates


def _mix_and_route(x, mod3, grp_rows, w, y, hm, gates):
    t, d = x.shape
    tm = min(_MM_ROWS, grp_rows, t)
    conv_g = _mm(y, w["w_conv_out"], col0=0, ncols=d, tm=tm, tn=1024, out_dtype=F32, mul=(gates, 0),
                 name="mm_conv_out")
    merged = _mm(hm, w["w_mlstm_out"], col0=0, ncols=d, tm=tm, tn=512, out_dtype=BF16, mul=(gates, d), add=conv_g,
                 name="mm_mlstm_out")
    x1 = _mm(merged, w["w_o"], col0=0, ncols=d, tm=tm, tn=1024, out_dtype=F32, res=(x, mod3, 2, grp_rows),
             name="mm_merge")
    h2 = _norm_mod(x1, w["norm2_g"], mod3, 3, 4, grp_rows, min(_NORM_ROWS, tm))
    pq = _mm(h2, w["peer_wq"], col0=0, ncols=w["peer_wq"].shape[1], tm=tm, tn=1024, out_dtype=F32, name="mm_peer_q")
    a_idx, b_idx, pg = _peer_topk(pq, w["peer_subkeys"], PEER_TOPK, tb=_PEER_TB)
    gmat = _peer_gates(a_idx, b_idx, pg, w["peer_subkeys"].shape[2], tg=LANES)
    return x1, h2, gmat


def kernel(x_prompt, x_sample, state_conv, state_mlstm_c, state_mlstm_n, state_mlstm_m, c_prompt, c_sample,
           ada_w, ada_b, norm1_g, norm2_g, w_in, mlstm_if_b, conv_dw, conv_db, conv_ln_g, conv_ln_b,
           w_conv_out, mlstm_norm_g, w_mlstm_out, w_o, peer_wq, peer_subkeys, peer_u, peer_v, final_g):
    bsz, seq, d = x_prompt.shape
    nb, dseq, _ = x_sample.shape
    assert dseq == 1
    assert ada_w.shape[0] == 1, "single-layer stack only"
    n_heads = mlstm_if_b.shape[1] // 2
    d_ml = mlstm_norm_g.shape[1]
    col_if = 2 * d + 4 * d_ml
    xp = x_prompt.reshape(bsz * seq, d)
    xs = x_sample.reshape(nb, d)
    n_c = nb + bsz
    c_all = jnp.pad(jnp.concatenate([c_sample, c_prompt], axis=0), ((0, (-n_c) % 16), (0, 0)))
    mod = _ada(c_all, ada_w[0], ada_b[0])
    mod_s = mod[:nb].reshape(1, nb, 6 * d)
    mod_p = mod[nb:n_c].reshape(bsz, 1, 6 * d)
    w_in_t = w_in[0].T
    w_if_t = jnp.pad(w_in_t[col_if:col_if + 2 * n_heads], ((0, LANES - 2 * n_heads), (0, 0)))
    w_gate_t = w_in_t[col_if + 2 * n_heads:]
    w = dict(norm1_g=norm1_g[0], norm2_g=norm2_g[0], w_in_t=w_in_t, w_if_t=w_if_t, w_gate_t=w_gate_t,
             w_conv_out=w_conv_out[0], mlstm_norm_g=mlstm_norm_g[0], w_mlstm_out=w_mlstm_out[0], w_o=w_o[0],
             peer_wq=peer_wq[0], peer_subkeys=peer_subkeys[0])
    conv_w = (conv_dw[0], conv_db[0], conv_ln_g[0], conv_ln_b[0])
    u_tab, v_tab = peer_u[0].astype(BF16), peer_v[0].astype(BF16)
    if_b, ml_g = mlstm_if_b[0], mlstm_norm_g[0]

    us, qkv_s, og_s, if_s, gates_s = _in_proj(xs, mod_s, nb, w, n_heads, F32)
    ys, buf_s_t = _conv_sample(state_conv[0].transpose(1, 0, 2), us, *conv_w)
    bs = buf_s_t.transpose(1, 0, 2)
    sample_step = (qkv_s, og_s, if_s, if_b, ml_g, state_mlstm_c[0], state_mlstm_n[0], state_mlstm_m[0], n_heads)

    up, qkv_p, og_p, if_p, gates_p = _in_proj(xp, mod_p, seq, w, n_heads, BF16)
    up3 = up.reshape(bsz, seq, d)
    yp = _conv_prompt(up3, *conv_w, tt=256).reshape(bsz * seq, d)
    bp = up3[:, seq - (conv_dw.shape[1] - 1):, :]
    hm_p, cp, np_, mp = _mlstm_prompt(qkv_p, og_p, if_p, if_b, ml_g, bsz, seq, n_heads, 256)
    xp, h2_p, gmat_p = _mix_and_route(xp, mod_p, seq, w, yp, hm_p, gates_p)

    n_steps = (bsz * seq // min(_PEER_TB, bsz * seq)) * (peer_subkeys.shape[3] // _PEER_KEYS_PER_STEP + 1)
    if n_steps >= n_heads * nb:
        pp, (hm_s, cs, ns, ms) = _peer_dense(h2_p, u_tab, v_tab, gmat_p, _PEER_TB, _PEER_KEYS_PER_STEP, sample_step)
    else:
        pp, _ = _peer_dense(h2_p, u_tab, v_tab, gmat_p, _PEER_TB, _PEER_KEYS_PER_STEP)
        hm_s, cs, ns, ms = _mlstm_sample(*sample_step)

    xs, h2_s, gmat_s = _mix_and_route(xs, mod_s, nb, w, ys, hm_s, gates_s)
    ps, _ = _peer_dense(h2_s, u_tab, v_tab, gmat_s, _PEER_TB, _G_KEYS)
    y_prompt = _final(xp, pp, mod_p, 5, final_g, seq, min(512, seq)).reshape(bsz, seq, d)
    y_sample = _final(xs, ps, mod_s, 5, final_g, nb, nb).reshape(nb, 1, d)
    return (y_prompt, y_sample) + tuple(o[None] for o in (bp, bs, cp, cs, np_, ns, mp, ms))
```

```python
import functools

import jax
import jax.numpy as jnp
from jax import lax
from jax.experimental import pallas as pl
from jax.experimental.pallas import tpu as pltpu

F32 = jnp.float32
BF16 = jnp.bfloat16
I32 = jnp.int32

EPS = 1e-6
NEG_INIT = -1e30
PEER_TOPK = 16
LANES = 128
VMEM_LIMIT = 56 * 1024 * 1024


def _cparams(*sem):
    return pltpu.CompilerParams(dimension_semantics=sem, vmem_limit_bytes=VMEM_LIMIT)


def _tile(target, *extents):
    t = min([target] + [e for e in extents if e])
    while any(e % t for e in extents):
        t -= LANES
    return t


def _sigmoid(x):
    return 1.0 / (1.0 + jnp.exp(-x))


def _log_sigmoid(x):
    return jnp.minimum(x, 0.0) - jnp.log1p(jnp.exp(-jnp.abs(x)))


def _split3(x):
    hi = x.astype(BF16)
    r1 = x - hi.astype(F32)
    mid = r1.astype(BF16)
    lo = (r1 - mid.astype(F32)).astype(BF16)
    return hi, mid, lo


def _ada_kernel(c_ref, w_ref, b_ref, o_ref):
    c = c_ref[...]
    a = (c * _sigmoid(c)).astype(BF16)
    o_ref[...] = jnp.dot(a, w_ref[...].astype(BF16), preferred_element_type=F32) + b_ref[...]


def _ada(c, w, b):
    m, d = c.shape
    n = w.shape[1]
    tn = _tile(1024, n)
    return pl.pallas_call(
        _ada_kernel,
        grid=(n // tn,),
        in_specs=[pl.BlockSpec((m, d), lambda j: (0, 0)),
                  pl.BlockSpec((d, tn), lambda j: (0, j)),
                  pl.BlockSpec((1, tn), lambda j: (0, j))],
        out_specs=pl.BlockSpec((m, tn), lambda j: (0, j)),
        out_shape=jax.ShapeDtypeStruct((m, n), F32),
        compiler_params=_cparams("arbitrary"),
        name="ada_mod",
    )(c, w, b.reshape(1, n))


def _norm_mod_kernel(x_ref, g_ref, sc_ref, sh_ref, o_ref):
    x = x_ref[...]
    y = x * lax.rsqrt(jnp.mean(x * x, axis=-1, keepdims=True) + EPS)
    o_ref[...] = (y * g_ref[...] * (1.0 + sc_ref[0]) + sh_ref[0]).astype(o_ref.dtype)


def _norm_mod(x, g, mod3, sh_idx, sc_idx, grp_rows, tm):
    t, d = x.shape
    r = mod3.shape[1]
    per = grp_rows // tm
    return pl.pallas_call(
        _norm_mod_kernel,
        grid=(t // tm,),
        in_specs=[pl.BlockSpec((tm, d), lambda i: (i, 0)),
                  pl.BlockSpec((1, d), lambda i: (0, 0)),
                  pl.BlockSpec((1, r, d), lambda i: (i // per, 0, sc_idx)),
                  pl.BlockSpec((1, r, d), lambda i: (i // per, 0, sh_idx))],
        out_specs=pl.BlockSpec((tm, d), lambda i: (i, 0)),
        out_shape=jax.ShapeDtypeStruct((t, d), BF16),
        compiler_params=_cparams("arbitrary"),
        name="norm_mod",
    )(x, g.reshape(1, d), mod3, mod3)


def _load_weight_tile(w_ref, transposed):
    w = w_ref[...]
    return (w.T if transposed else w).astype(BF16)


def _mm_kernel(*refs, act, has_mul, has_add, has_res, has_cast, w_transposed):
    a_ref, w_ref = refs[0], refs[1]
    wb_ref = refs[-1]
    if has_cast:
        cast_in_ref, o_ref, cast_out_ref = refs[-4], refs[-3], refs[-2]
        extra = list(refs[2:-4])
    else:
        o_ref = refs[-2]
        extra = list(refs[2:-2])

    @pl.when(pl.program_id(1) == 0)
    def _():
        wb_ref[...] = _load_weight_tile(w_ref, w_transposed)

    if has_cast:
        cast_out_ref[...] = cast_in_ref[...].astype(BF16)

    acc = jnp.dot(a_ref[...], wb_ref[...], preferred_element_type=F32)
    if act == "sigmoid":
        acc = _sigmoid(acc)
    if has_mul:
        acc = acc * extra.pop(0)[...]
    if has_add:
        acc = acc + extra.pop(0)[...]
    if has_res:
        x_ref, g_ref = extra
        acc = x_ref[...] + g_ref[0] * acc
    o_ref[...] = acc.astype(o_ref.dtype)


def _mm(a, w, *, col0, ncols, tm, tn, out_dtype, act=None, mul=None, add=None, res=None, w_transposed=False,
        cast=None, name="mm"):
    m, k = a.shape
    tm = min(tm, m)
    tn = _tile(tn, ncols, col0, mul[1] if mul else 0)
    cb = col0 // tn
    nj = ncols // tn
    w_spec = (pl.BlockSpec((tn, k), lambda j, i: (cb + j, 0)) if w_transposed
              else pl.BlockSpec((k, tn), lambda j, i: (0, cb + j)))
    in_specs = [pl.BlockSpec((tm, k), lambda j, i: (i, 0)), w_spec]
    operands = [a, w]
    tile = pl.BlockSpec((tm, tn), lambda j, i: (i, j))
    if mul is not None:
        mb = mul[1] // tn
        in_specs.append(pl.BlockSpec((tm, tn), lambda j, i: (i, mb + j)))
        operands.append(mul[0])
    if add is not None:
        in_specs.append(tile)
        operands.append(add)
    if res is not None:
        x, mod3, g_idx, grp_rows = res
        per = grp_rows // tm
        in_specs += [tile, pl.BlockSpec((1, mod3.shape[1], tn), lambda j, i: (i // per, 0, g_idx * nj + j))]
        operands += [x, mod3]
    ni = m // tm
    out_specs = [tile]
    out_shape = [jax.ShapeDtypeStruct((m, ncols), out_dtype)]
    if cast is not None:
        rows, cols = cast.shape
        rb = rows // (nj * ni)
        assert rb * nj * ni == rows and rb % 16 == 0
        cast_spec = pl.BlockSpec((rb, cols), lambda j, i: (j * ni + i, 0))
        in_specs.append(cast_spec)
        operands.append(cast)
        out_specs.append(cast_spec)
        out_shape.append(jax.ShapeDtypeStruct((rows, cols), BF16))
    outs = pl.pallas_call(
        functools.partial(_mm_kernel, act=act, has_mul=mul is not None, has_add=add is not None,
                          has_res=res is not None, has_cast=cast is not None, w_transposed=w_transposed),
        grid=(nj, ni),
        in_specs=in_specs,
        out_specs=out_specs,
        out_shape=out_shape,
        scratch_shapes=[pltpu.VMEM((k, tn), BF16)],
        compiler_params=_cparams("arbitrary", "arbitrary"),
        name=name,
    )(*operands)
    return outs[0] if cast is None else tuple(outs)


def _mm_glu_kernel(a_ref, wa_ref, wg_ref, o_ref, wab_ref, wgb_ref):
    @pl.when(pl.program_id(1) == 0)
    def _():
        wab_ref[...] = _load_weight_tile(wa_ref, True)
        wgb_ref[...] = _load_weight_tile(wg_ref, True)

    a = a_ref[...]
    lin = jnp.dot(a, wab_ref[...], preferred_element_type=F32)
    gate = jnp.dot(a, wgb_ref[...], preferred_element_type=F32)
    o_ref[...] = lin * _sigmoid(gate)


def _mm_glu(a, w_t, *, d_half, tm, tn):
    m, k = a.shape
    tm = min(tm, m)
    tn = _tile(tn, d_half)
    off = d_half // tn
    return pl.pallas_call(
        _mm_glu_kernel,
        grid=(d_half // tn, m // tm),
        in_specs=[pl.BlockSpec((tm, k), lambda j, i: (i, 0)),
                  pl.BlockSpec((tn, k), lambda j, i: (j, 0)),
                  pl.BlockSpec((tn, k), lambda j, i: (off + j, 0))],
        out_specs=pl.BlockSpec((tm, tn), lambda j, i: (i, j)),
        out_shape=jax.ShapeDtypeStruct((m, d_half), F32),
        scratch_shapes=[pltpu.VMEM((k, tn), BF16), pltpu.VMEM((k, tn), BF16)],
        compiler_params=_cparams("arbitrary", "arbitrary"),
        name="mm_glu",
    )(a, w_t, w_t)


_HALO = 32
_CONV_RC = 64


def _ln_silu(y, lg, lb):
    mu = jnp.mean(y, axis=-1, keepdims=True)
    yc = y - mu
    var = jnp.mean(yc * yc, axis=-1, keepdims=True)
    z = yc * lax.rsqrt(var + EPS) * lg + lb
    return z * _sigmoid(z)


def _conv_prompt_kernel(u_ref, dw_ref, db_ref, lg_ref, lb_ref, o_ref, win_ref, y_ref, sh_ref, *, tt, kw):
    d = u_ref.shape[2]
    t = pl.program_id(1)

    @pl.when(t == 0)
    def _():
        win_ref[0:_HALO, :] = jnp.zeros((_HALO, d), F32)

    @pl.when(t > 0)
    def _():
        win_ref[0:_HALO, :] = win_ref[tt:tt + _HALO, :]

    win_ref[_HALO:_HALO + tt, :] = u_ref[0]
    base = _HALO - (kw - 1)
    rc = min(_CONV_RC, tt)
    for mis in range(1, 8):
        sh_ref[mis - 1] = win_ref[mis:mis + tt + _HALO - 8, :]

    def row_chunk(r, carry):
        r0 = pl.multiple_of(r * rc, rc)
        for c in range(d // LANES):
            ls = slice(c * LANES, (c + 1) * LANES)
            acc = jnp.broadcast_to(db_ref[:, ls], (rc, LANES))
            for mis in range(8):
                taps = [j for j in range(kw) if (base + j) % 8 == mis]
                span = max(base + j - mis for j in taps) + rc
                w = win_ref[pl.ds(r0, span), ls] if mis == 0 else sh_ref[mis - 1, pl.ds(r0, span), ls]
                for j in taps:
                    off = base + j - mis
                    acc = acc + w[off:off + rc, :] * dw_ref[j:j + 1, ls]
            y_ref[pl.ds(r0, rc), ls] = acc
        return carry

    lax.fori_loop(0, tt // rc, row_chunk, 0)
    o_ref[0] = _ln_silu(y_ref[...], lg_ref[...], lb_ref[...]).astype(o_ref.dtype)


def _conv_prompt(u3, dw, db, lg, lb, tt):
    b, t, d = u3.shape
    tt = min(tt, t)
    kw = dw.shape[0]
    assert kw - 1 <= _HALO <= tt
    return pl.pallas_call(
        functools.partial(_conv_prompt_kernel, tt=tt, kw=kw),
        grid=(b, t // tt),
        in_specs=[pl.BlockSpec((1, tt, d), lambda i, j: (i, j, 0)),
                  pl.BlockSpec((kw, d), lambda i, j: (0, 0)),
                  pl.BlockSpec((1, d), lambda i, j: (0, 0)),
                  pl.BlockSpec((1, d), lambda i, j: (0, 0)),
                  pl.BlockSpec((1, d), lambda i, j: (0, 0))],
        out_specs=pl.BlockSpec((1, tt, d), lambda i, j: (i, j, 0)),
        out_shape=jax.ShapeDtypeStruct((b, t, d), BF16),
        scratch_shapes=[pltpu.VMEM((_HALO + tt, d), F32), pltpu.VMEM((tt, d), F32),
                        pltpu.VMEM((7, tt + _HALO - 8, d), F32)],
        compiler_params=_cparams("arbitrary", "arbitrary"),
        name="conv_prompt",
    )(u3, dw, db.reshape(1, d), lg.reshape(1, d), lb.reshape(1, d))


def _conv_sample_kernel(buf_ref, u_ref, dw_ref, db_ref, lg_ref, lb_ref, o_ref, nb_ref, *, kw):
    u = u_ref[...]
    acc = u * dw_ref[kw - 1:kw, :] + db_ref[...]
    for j in range(kw - 1):
        acc = acc + buf_ref[j] * dw_ref[j:j + 1, :]
    o_ref[...] = _ln_silu(acc, lg_ref[...], lb_ref[...]).astype(o_ref.dtype)
    for j in range(kw - 2):
        nb_ref[j] = buf_ref[j + 1]
    nb_ref[kw - 2] = u


def _conv_sample(buf_t, u, dw, db, lg, lb, bb=16):
    b, d = u.shape
    kw = dw.shape[0]
    bb = min(bb, b)
    return pl.pallas_call(
        functools.partial(_conv_sample_kernel, kw=kw),
        grid=(b // bb,),
        in_specs=[pl.BlockSpec((kw - 1, bb, d), lambda i: (0, i, 0)),
                  pl.BlockSpec((bb, d), lambda i: (i, 0)),
                  pl.BlockSpec((kw, d), lambda i: (0, 0)),
                  pl.BlockSpec((1, d), lambda i: (0, 0)),
                  pl.BlockSpec((1, d), lambda i: (0, 0)),
                  pl.BlockSpec((1, d), lambda i: (0, 0))],
        out_specs=[pl.BlockSpec((bb, d), lambda i: (i, 0)),
                   pl.BlockSpec((kw - 1, bb, d), lambda i: (0, i, 0))],
        out_shape=[jax.ShapeDtypeStruct((b, d), BF16),
                   jax.ShapeDtypeStruct((kw - 1, b, d), F32)],
        compiler_params=_cparams("arbitrary"),
        name="conv_sample",
    )(buf_t, u, dw, db.reshape(1, d), lg.reshape(1, d), lb.reshape(1, d))


_MLSTM_CB = 256


def _mlstm_prompt_kernel(bias_ref, q_ref, k_ref, v_ref, ifc_ref, ifr_ref, og_ref, ng_ref,
                         h_ref, c_ref, n_ref, m_ref, cb_ref, ns_ref, ms_ref, *, n_heads):
    hd = pl.program_id(1)
    ck = pl.program_id(2)
    L, dk = q_ref.shape
    dv = v_ref.shape[1]
    scale = dk ** -0.5

    @pl.when(ck == 0)
    def _():
        c_ref[...] = jnp.zeros(c_ref.shape, F32)
        cb_ref[...] = jnp.zeros(cb_ref.shape, BF16)
        ns_ref[...] = jnp.zeros(ns_ref.shape, F32)
        ms_ref[...] = jnp.full(ms_ref.shape, NEG_INIT, F32)

    q = q_ref[...]
    k = k_ref[...]
    v = v_ref[...]
    bi = bias_ref[hd]
    bf = bias_ref[n_heads + hd]
    col = ifc_ref[0, 0]
    li_c = col[:, 0:1] + bi
    lf_c = _log_sigmoid(col[:, 1:2] + bf)
    row = ifr_ref[0, 0]
    li_r = row[0:1, :] + bi
    lf_r = _log_sigmoid(row[1:2, :] + bf)

    r_io = lax.broadcasted_iota(I32, (L, L), 0)
    c_io = lax.broadcasted_iota(I32, (L, L), 1)
    causal = r_io >= c_io
    tri = jnp.where(causal, 1.0, 0.0).astype(BF16)
    tri_t = jnp.where(r_io <= c_io, 1.0, 0.0).astype(BF16)
    b_c = sum(jnp.dot(tri, p, preferred_element_type=F32)
              for p in _split3(jnp.broadcast_to(lf_c, (L, LANES))))[:, 0:1]
    b_r = sum(jnp.dot(p, tri_t, preferred_element_type=F32)
              for p in _split3(jnp.broadcast_to(lf_r, (16, L))))[0:1, :]

    m_prev = ms_ref[0:1, 0:1]
    b_last = b_c[L - 1:L, :]
    dmat = jnp.where(causal, b_c - (b_r - li_r), -jnp.inf)
    inter = b_c + m_prev
    m_t = jnp.maximum(inter, jnp.max(dmat, axis=-1, keepdims=True))
    w_inter = jnp.exp(inter - m_t)
    s = lax.dot_general(q, k, (((1,), (1,)), ((), ())), preferred_element_type=F32)
    s = s * scale * jnp.exp(dmat - m_t)
    qc = jnp.dot(q, cb_ref[...], preferred_element_type=F32)
    num = w_inter * qc + jnp.dot(s.astype(BF16), v, preferred_element_type=F32)
    qn = jnp.sum(q.astype(F32) * ns_ref[...], axis=-1, keepdims=True)
    den = w_inter * qn + jnp.sum(s, axis=-1, keepdims=True)
    h = num / jnp.maximum(jnp.abs(den), jnp.exp(-m_t))
    hn = h * lax.rsqrt(jnp.mean(h * h, axis=-1, keepdims=True) + EPS)
    h_ref[...] = (hn * ng_ref[...] * og_ref[...]).astype(h_ref.dtype)

    g_c = b_last - b_c + li_c
    m_new = jnp.maximum(b_last + m_prev, jnp.max(g_c, axis=0, keepdims=True))
    w_s = jnp.exp(g_c - m_new)
    decay = jnp.exp(b_last + m_prev - m_new)
    kw = k.astype(F32) * (w_s * scale)
    kwb = kw.astype(BF16)
    cbw = min(_MLSTM_CB, dv)
    for j in range(dv // cbw):
        cs = slice(j * cbw, (j + 1) * cbw)
        kv = lax.dot_general(kwb, v[:, cs], (((0,), (0,)), ((), ())), preferred_element_type=F32)
        c_new = decay * c_ref[0, 0, :, cs] + kv
        c_ref[0, 0, :, cs] = c_new
        cb_ref[:, cs] = c_new.astype(BF16)
    n_new = decay * ns_ref[...] + jnp.sum(kw, axis=0, keepdims=True)
    ns_ref[...] = n_new
    ms_ref[...] = jnp.broadcast_to(m_new, ms_ref.shape)
    n_ref[0, 0] = n_new
    m_ref[0, 0] = jnp.broadcast_to(m_new, (1, LANES))


def _mlstm_prompt(qkv, og, if_pre, if_b, norm_g, batch, seq, n_heads, chunk):
    dh = qkv.shape[1] // (3 * n_heads)
    chunk = min(chunk, seq)
    nc = seq // chunk
    ifx = if_pre.reshape(batch, seq, 2, n_heads)
    ifc = ifx.transpose(0, 3, 1, 2)
    ifr = ifx.transpose(0, 3, 2, 1)
    row = lambda b, h, c: (b * nc + c)
    h_out, c_out, n_out, m_out = pl.pallas_call(
        functools.partial(_mlstm_prompt_kernel, n_heads=n_heads),
        grid=(batch, n_heads, nc),
        in_specs=[pl.BlockSpec(memory_space=pltpu.SMEM),
                  pl.BlockSpec((chunk, dh), lambda b, h, c: (row(b, h, c), h)),
                  pl.BlockSpec((chunk, dh), lambda b, h, c: (row(b, h, c), n_heads + h)),
                  pl.BlockSpec((chunk, dh), lambda b, h, c: (row(b, h, c), 2 * n_heads + h)),
                  pl.BlockSpec((1, 1, chunk, 2), lambda b, h, c: (b, h, c, 0)),
                  pl.BlockSpec((1, 1, 2, chunk), lambda b, h, c: (b, h, 0, c)),
                  pl.BlockSpec((chunk, dh), lambda b, h, c: (row(b, h, c), h)),
                  pl.BlockSpec((1, dh), lambda b, h, c: (0, h))],
        out_specs=[pl.BlockSpec((chunk, dh), lambda b, h, c: (row(b, h, c), h)),
                   pl.BlockSpec((1, 1, dh, dh), lambda b, h, c: (b, h, 0, 0)),
                   pl.BlockSpec((1, 1, 1, dh), lambda b, h, c: (b, h, 0, 0)),
                   pl.BlockSpec((1, 1, 1, LANES), lambda b, h, c: (b, h, 0, 0))],
        out_shape=[jax.ShapeDtypeStruct((batch * seq, n_heads * dh), BF16),
                   jax.ShapeDtypeStruct((batch, n_heads, dh, dh), F32),
                   jax.ShapeDtypeStruct((batch, n_heads, 1, dh), F32),
                   jax.ShapeDtypeStruct((batch, n_heads, 1, LANES), F32)],
        scratch_shapes=[pltpu.VMEM((dh, dh), BF16), pltpu.VMEM((1, dh), F32), pltpu.VMEM((8, LANES), F32)],
        compiler_params=_cparams("arbitrary", "arbitrary", "arbitrary"),
        name="mlstm_prompt",
    )(if_b, qkv, qkv, qkv, ifc, ifr, og, norm_g.reshape(1, -1))
    return h_out, c_out, n_out[:, :, 0, :], m_out[:, :, 0, 0]


def _mlstm_sample_head_setup(b, q_ref, k_ref, qt_ref, kt_ref):
    @pl.when(b == 0)
    def _():
        for src, dst in ((q_ref, qt_ref), (k_ref, kt_ref)):
            for i, p in enumerate(_split3(src[...].T)):
                dst[i] = p


def _mlstm_sample_columns(b, qt_ref, kt_ref):
    nb = qt_ref.shape[2]
    sel = jnp.where(lax.broadcasted_iota(I32, (nb, LANES), 0) == b, 1.0, 0.0).astype(BF16)
    return tuple(sum(jnp.dot(t_ref[i], sel, preferred_element_type=F32) for i in range(3))
                 for t_ref in (qt_ref, kt_ref))


def _mlstm_sample_item(hd, b, columns, bias_ref, ifp_ref, mst_ref, q_ref, k_ref, v_ref, c_ref, n_ref, og_ref,
                       ng_ref, h_ref, co_ref, no_ref, mo_ref, qt_ref, kt_ref, *, n_heads):
    nb, dk = q_ref.shape
    dv = v_ref.shape[1]
    scale = dk ** -0.5
    qb, kb = columns
    li = jnp.full((1, 1), ifp_ref[b, hd] + bias_ref[hd], F32)
    lf = _log_sigmoid(jnp.full((1, 1), ifp_ref[b, n_heads + hd] + bias_ref[n_heads + hd], F32))
    m_prev = jnp.full((1, 1), mst_ref[b, hd], F32)
    m_t = jnp.maximum(lf + m_prev, li)
    decay = jnp.exp(lf + m_prev - m_t)
    w_s = jnp.exp(li - m_t)

    q_row = q_ref[pl.ds(b, 1), :]
    k_row = k_ref[pl.ds(b, 1), :]
    v_row = v_ref[pl.ds(b, 1), :]
    n_row = n_ref[0, 0]
    sval = jnp.sum(q_row * k_row, axis=-1, keepdims=True) * scale * w_s
    den = decay * jnp.sum(q_row * n_row, axis=-1, keepdims=True) + sval
    denom = jnp.maximum(jnp.abs(den), jnp.exp(-m_t))
    wk = w_s * scale
    pieces = []
    for j in range(dv // LANES):
        ls = slice(j * LANES, (j + 1) * LANES)
        cj = c_ref[0, 0, :, ls]
        vj = v_row[:, ls]
        qc = jnp.sum(qb * cj, axis=0, keepdims=True)
        co_ref[0, 0, :, ls] = decay * cj + kb * (wk * vj)
        pieces.append((decay * qc + sval * vj) / denom)
    h = jnp.concatenate(pieces, axis=1)
    hn = h * lax.rsqrt(jnp.mean(h * h, axis=-1, keepdims=True) + EPS)
    h_ref[0] = (hn * ng_ref[...] * og_ref[0]).astype(h_ref.dtype)
    no_ref[0, 0] = decay * n_row + wk * k_row
    mo_ref[0, 0] = jnp.broadcast_to(m_t, (1, LANES))


def _mlstm_sample_kernel(*refs, n_heads):
    b = pl.program_id(1)
    _mlstm_sample_head_setup(b, refs[3], refs[4], refs[-2], refs[-1])
    _mlstm_sample_item(pl.program_id(0), b, _mlstm_sample_columns(b, refs[-2], refs[-1]), *refs, n_heads=n_heads)


def _mlstm_sample_plumbing(qkv, og, if_pre, if_b, norm_g, st_c, st_n, st_m, n_heads, item):
    nb = qkv.shape[0]
    dh = qkv.shape[1] // (3 * n_heads)

    def at(fn):
        return lambda *ids: fn(*item(*ids))

    smem = pl.BlockSpec(memory_space=pltpu.SMEM)
    in_specs = [smem, smem, smem,
                pl.BlockSpec((nb, dh), at(lambda h, b: (0, h))),
                pl.BlockSpec((nb, dh), at(lambda h, b: (0, n_heads + h))),
                pl.BlockSpec((nb, dh), at(lambda h, b: (0, 2 * n_heads + h))),
                pl.BlockSpec((1, 1, dh, dh), at(lambda h, b: (b, h, 0, 0))),
                pl.BlockSpec((1, 1, 1, dh), at(lambda h, b: (b, h, 0, 0))),
                pl.BlockSpec((1, 1, dh), at(lambda h, b: (b, 0, h))),
                pl.BlockSpec((1, dh), at(lambda h, b: (0, h)))]
    operands = [if_b, if_pre, st_m, qkv, qkv, qkv, st_c, st_n.reshape(nb, n_heads, 1, dh),
                og.reshape(nb, 1, n_heads * dh), norm_g.reshape(1, -1)]
    out_specs = [pl.BlockSpec((1, 1, dh), at(lambda h, b: (b, 0, h))),
                 pl.BlockSpec((1, 1, dh, dh), at(lambda h, b: (b, h, 0, 0))),
                 pl.BlockSpec((1, 1, 1, dh), at(lambda h, b: (b, h, 0, 0))),
                 pl.BlockSpec((1, 1, 1, LANES), at(lambda h, b: (b, h, 0, 0)))]
    out_shape = [jax.ShapeDtypeStruct((nb, 1, n_heads * dh), BF16),
                 jax.ShapeDtypeStruct((nb, n_heads, dh, dh), F32),
                 jax.ShapeDtypeStruct((nb, n_heads, 1, dh), F32),
                 jax.ShapeDtypeStruct((nb, n_heads, 1, LANES), F32)]
    scratch = [pltpu.VMEM((3, dh, nb), BF16), pltpu.VMEM((3, dh, nb), BF16)]
    return operands, in_specs, out_specs, out_shape, scratch


def _mlstm_sample_results(h_out, c_out, n_out, m_out):
    return h_out.reshape(h_out.shape[0], -1), c_out, n_out[:, :, 0, :], m_out[:, :, 0, 0]


def _mlstm_sample(qkv, og, if_pre, if_b, norm_g, st_c, st_n, st_m, n_heads):
    operands, in_specs, out_specs, out_shape, scratch = _mlstm_sample_plumbing(
        qkv, og, if_pre, if_b, norm_g, st_c, st_n, st_m, n_heads, lambda h, b: (h, b))
    outs = pl.pallas_call(
        functools.partial(_mlstm_sample_kernel, n_heads=n_heads),
        grid=(n_heads, qkv.shape[0]),
        in_specs=in_specs, out_specs=out_specs, out_shape=out_shape, scratch_shapes=scratch,
        compiler_params=_cparams("arbitrary", "arbitrary"),
        name="mlstm_sample",
    )(*operands)
    return _mlstm_sample_results(*outs)


def _topk_rows(problems, n_take):
    state = []
    for s, payload in problems:
        iota = lax.broadcasted_iota(I32, s.shape, 0).astype(F32)
        state.append([s, iota, payload, [], []])
    for _ in range(n_take):
        for st in state:
            s, iota, payload, vals, picks = st
            m = jnp.max(s, axis=0, keepdims=True)
            pos = jnp.min(jnp.where(s == m, iota, float(s.shape[0])), axis=0, keepdims=True)
            hit = iota == pos
            vals.append(m)
            picks.append(pos if payload is None
                         else jnp.max(jnp.where(hit, payload, -1.0), axis=0, keepdims=True))
            st[0] = jnp.where(hit, -jnp.inf, s)
    return [(st[3], st[4]) for st in state]


def _peer_pairs(topk):
    return [(i, j) for i in range(topk) for j in range(topk) if (i + 1) * (j + 1) <= topk]


def _peer_topk_kernel(q_ref, sk_ref, a_ref, b_ref, g_ref, cv_ref, ce_ref, *, topk):
    tb = q_ref.shape[0]
    nk, dsub = sk_ref.shape[2], sk_ref.shape[3]
    pairs = _peer_pairs(topk)
    sk = [sk_ref[0, p].astype(BF16) for p in range(2)]

    n_groups = tb // LANES
    per_iter = cv_ref.shape[0]

    def groups(gi, carry):
        starts = [pl.multiple_of((gi * per_iter + n) * LANES, LANES) for n in range(per_iter)]
        for n, t0 in enumerate(starts):
            scores = []
            for p in range(2):
                qp = q_ref[pl.ds(t0, LANES), p * dsub:(p + 1) * dsub].astype(BF16)
                scores.append(lax.dot_general(sk[p], qp, (((1,), (1,)), ((), ())),
                                              preferred_element_type=F32))
            (sv0, si0), (sv1, si1) = _topk_rows([(scores[0], None), (scores[1], None)], topk)
            cv_ref[n] = jnp.full(cv_ref.shape[1:], -jnp.inf, F32)
            ce_ref[n] = jnp.zeros(ce_ref.shape[1:], F32)
            for r, (i, j) in enumerate(pairs):
                cv_ref[n, r:r + 1, :] = sv0[i] + sv1[j]
                ce_ref[n, r:r + 1, :] = si0[i] * float(nk) + si1[j]
        picked = _topk_rows([(cv_ref[n], ce_ref[n]) for n in range(per_iter)], topk)
        for t0, (tv, te) in zip(starts, picked):
            ex = [jnp.exp(x - tv[0]) for x in tv]
            tot = sum(ex)
            for r in range(topk):
                first = jnp.floor(te[r] * (1.0 / nk))
                a_ref[r:r + 1, pl.ds(t0, LANES)] = first.astype(I32)
                b_ref[r:r + 1, pl.ds(t0, LANES)] = (te[r] - first * float(nk)).astype(I32)
                g_ref[r:r + 1, pl.ds(t0, LANES)] = ex[r] / tot
        return carry

    lax.fori_loop(0, n_groups // per_iter, groups, 0)


def _peer_topk(q, subkeys, topk, tb):
    t = q.shape[0]
    nh, _, nk, dsub = subkeys.shape
    tb = min(tb, t)
    assert nk & (nk - 1) == 0 and nk * nk < 2 ** 24
    n_pad = -(-len(_peer_pairs(topk)) // 8) * 8
    per_iter = max(n for n in (4, 2, 1) if (tb // LANES) % n == 0)
    out = jax.ShapeDtypeStruct((nh * topk, t), I32)
    return pl.pallas_call(
        functools.partial(_peer_topk_kernel, topk=topk),
        grid=(t // tb, nh),
        in_specs=[pl.BlockSpec((tb, 2 * dsub), lambda i, h: (i, h)),
                  pl.BlockSpec((1, 2, nk, dsub), lambda i, h: (h, 0, 0, 0))],
        out_specs=[pl.BlockSpec((topk, tb), lambda i, h: (h, i))] * 3,
        out_shape=[out, out, jax.ShapeDtypeStruct((nh * topk, t), F32)],
        scratch_shapes=[pltpu.VMEM((per_iter, n_pad, LANES), F32)] * 2,
        compiler_params=_cparams("arbitrary", "arbitrary"),
        name="peer_topk",
    )(q, subkeys)


def _peer_gates_kernel(a_ref, b_ref, g_ref, o_ref, at_ref, bt_ref, gt_ref):
    nk = o_ref.shape[1]
    tg = o_ref.shape[0]
    at_ref[...] = a_ref[...].T
    bt_ref[...] = b_ref[...].T
    gt_ref[...] = g_ref[...].T
    key = lax.broadcasted_iota(I32, (nk, at_ref.shape[1]), 0)

    def token(t, carry):
        a = at_ref[pl.ds(t, 1), :]
        b = bt_ref[pl.ds(t, 1), :]
        g = gt_ref[pl.ds(t, 1), :]
        first = jnp.where(key == a, g, 0.0).astype(BF16)
        second = jnp.where(key == b, 1.0, 0.0).astype(BF16)
        o_ref[t] = lax.dot_general(first, second, (((1,), (1,)), ((), ())), preferred_element_type=F32)
        return carry

    lax.fori_loop(0, tg, token, 0, unroll=16)


def _peer_gates(a_idx, b_idx, gates, nk, tg):
    hk, t = a_idx.shape
    tg = min(tg, t)
    return pl.pallas_call(
        _peer_gates_kernel,
        grid=(t // tg,),
        in_specs=[pl.BlockSpec((hk, tg), lambda i: (0, i))] * 3,
        out_specs=pl.BlockSpec((tg, nk, nk), lambda i: (i, 0, 0)),
        out_shape=jax.ShapeDtypeStruct((t, nk, nk), F32),
        scratch_shapes=[pltpu.VMEM((tg, hk), I32), pltpu.VMEM((tg, hk), I32), pltpu.VMEM((tg, hk), F32)],
        compiler_params=_cparams("arbitrary"),
        name="peer_gates",
    )(a_idx, b_idx, gates)


_SQRT_HALF = 0.7071067811865476


_G_KEYS = 8
_N_RIDER_IN = 10
_N_RIDER_OUT = 4


def _peer_dense_kernel(*refs, n_blocks, keys_per_step, rider):
    x_ref, u_ref, v_ref, g_ref = refs[:4]
    if rider is None:
        o_ref, coef_ref = refs[4:]
    else:
        rider_in = refs[4:4 + _N_RIDER_IN]
        o_ref = refs[4 + _N_RIDER_IN]
        rider_out = refs[5 + _N_RIDER_IN:5 + _N_RIDER_IN + _N_RIDER_OUT]
        coef_ref = refs[5 + _N_RIDER_IN + _N_RIDER_OUT]
        rider_scratch = refs[6 + _N_RIDER_IN + _N_RIDER_OUT:]
    e = pl.program_id(1)
    cur = e % 2
    prev = 1 - cur

    @pl.when(e == 0)
    def _():
        o_ref[...] = jnp.zeros(o_ref.shape, F32)
        coef_ref[1] = jnp.zeros(coef_ref.shape[1:], BF16)

    if rider is not None:
        n_heads, n_items, nb = rider
        item = jnp.minimum(pl.program_id(0) * (n_blocks + 1) + e, n_items - 1)
        _mlstm_sample_head_setup(item % nb, rider_in[3], rider_in[4], *rider_scratch)
        columns = _mlstm_sample_columns(item % nb, *rider_scratch)

    a = lax.dot_general(x_ref[...], u_ref[...], (((1,), (1,)), ((), ())), preferred_element_type=F32)
    o_ref[...] += jnp.dot(coef_ref[prev], v_ref[...], preferred_element_type=F32)
    if rider is not None:
        _mlstm_sample_item(item // nb, item % nb, columns, *rider_in, *rider_out, *rider_scratch, n_heads=n_heads)
    act = 0.5 * a * (1.0 + lax.erf(a * _SQRT_HALF))
    per_block = g_ref.shape[1] // keys_per_step
    first = 0 if per_block == 1 else (jnp.minimum(e, n_blocks - 1) % per_block) * keys_per_step
    g = jnp.concatenate([g_ref[:, first + s, :] for s in range(keys_per_step)], axis=1)
    coef_ref[cur] = (g * act).astype(BF16)


def _peer_dense(x, u_tab, v_tab, gmat, tb, keys_per_step, rider_args=None):
    t, d = x.shape
    nk = gmat.shape[1]
    tb = min(tb, t)
    eb = keys_per_step * nk
    ne = nk // keys_per_step
    per_block = _G_KEYS // keys_per_step
    cur_blk = lambda e: jnp.minimum(e, ne - 1)
    once = pl.Buffered(1)
    in_specs = [pl.BlockSpec((tb, d), lambda i, e: (i, 0), pipeline_mode=once),
                pl.BlockSpec((eb, d), lambda i, e: (cur_blk(e), 0)),
                pl.BlockSpec((eb, d), lambda i, e: (jnp.maximum(e - 1, 0), 0)),
                pl.BlockSpec((tb, _G_KEYS, nk), lambda i, e: (i, cur_blk(e) // per_block, 0))]
    operands = [x, u_tab, v_tab, gmat]
    out_specs = [pl.BlockSpec((tb, d), lambda i, e: (i, 0), pipeline_mode=once)]
    out_shape = [jax.ShapeDtypeStruct((t, d), F32)]
    scratch = [pltpu.VMEM((2, tb, eb), BF16)]
    rider = None
    if rider_args is not None:
        n_heads = rider_args[-1]
        nb = rider_args[0].shape[0]
        n_items = n_heads * nb
        assert (t // tb) * (ne + 1) >= n_items

        def item(i, e):
            k = jnp.minimum(i * (ne + 1) + e, n_items - 1)
            return k // nb, k % nb

        r_ops, r_in, r_out, r_shape, r_scratch = _mlstm_sample_plumbing(*rider_args, item)
        assert len(r_ops) == _N_RIDER_IN and len(r_out) == _N_RIDER_OUT
        operands += r_ops
        in_specs += r_in
        out_specs += r_out
        out_shape += r_shape
        scratch += r_scratch
        rider = (n_heads, n_items, nb)
    outs = pl.pallas_call(
        functools.partial(_peer_dense_kernel, n_blocks=ne, keys_per_step=keys_per_step, rider=rider),
        grid=(t // tb, ne + 1),
        in_specs=in_specs, out_specs=out_specs, out_shape=out_shape, scratch_shapes=scratch,
        compiler_params=_cparams("arbitrary", "arbitrary"),
        name="peer_dense" if rider is None else "peer_dense_mlstm_sample",
    )(*operands)
    return outs[0], (None if rider is None else _mlstm_sample_results(*outs[1:]))


def _final_kernel(x_ref, p_ref, g_ref, fg_ref, o_ref):
    x = x_ref[...] + g_ref[0] * p_ref[...]
    o_ref[...] = x * lax.rsqrt(jnp.mean(x * x, axis=-1, keepdims=True) + EPS) * fg_ref[...]


def _final(x, p, mod3, g_idx, final_g, grp_rows, tm):
    t, d = x.shape
    r = mod3.shape[1]
    per = grp_rows // tm
    return pl.pallas_call(
        _final_kernel,
        grid=(t // tm,),
        in_specs=[pl.BlockSpec((tm, d), lambda i: (i, 0)),
                  pl.BlockSpec((tm, d), lambda i: (i, 0)),
                  pl.BlockSpec((1, r, d), lambda i: (i // per, 0, g_idx)),
                  pl.BlockSpec((1, d), lambda i: (0, 0))],
        out_specs=pl.BlockSpec((tm, d), lambda i: (i, 0)),
        out_shape=jax.ShapeDtypeStruct((t, d), F32),
        compiler_params=_cparams("arbitrary"),
        name="final_norm",
    )(x, p, mod3, final_g.reshape(1, d))


_MM_ROWS = 1024
_NORM_ROWS = 512
_PEER_TB = 1024
_PEER_KEYS_PER_STEP = 2


def _in_proj(x, mod3, grp_rows, w, n_heads, qkv_dtype, tables=None):
    t, d = x.shape
    d_ml = w["mlstm_norm_g"].shape[0]
    tm = min(_MM_ROWS, grp_rows, t)
    h1 = _norm_mod(x, w["norm1_g"], mod3, 0, 1, grp_rows, min(_NORM_ROWS, tm))
    u = _mm_glu(h1, w["w_in_t"], d_half=d, tm=tm, tn=512)
    proj = functools.partial(_mm, h1, tm=tm, w_transposed=True)
    qkv = proj(w["w_in_t"], col0=2 * d, ncols=3 * d_ml, tn=1024, out_dtype=qkv_dtype, name="mm_qkv")
    og = proj(w["w_in_t"], col0=2 * d + 3 * d_ml, ncols=d_ml, tn=1024, out_dtype=F32, act="sigmoid",
              cast=tables and tables[0], name="mm_ogate")
    if_pre = proj(w["w_if_t"], col0=0, ncols=LANES, tn=LANES, out_dtype=F32, name="mm_if")[:, :2 * n_heads]
    gates = proj(w["w_gate_t"], col0=0, ncols=2 * d, tn=1024, out_dtype=F32, act="sigmoid",
                 cast=tables and tables[1], name="mm_gates")
    if tables is None:
        return u, qkv, og, if_pre, gates
    return u, qkv, og[0], if_pre, gates[0], (og[1], gates[1])


def _mix_and_route(x, mod3, grp_rows, w, y, hm, gates):
    t, d = x.shape
    tm = min(_MM_ROWS, grp_rows, t)
    conv_g = _mm(y, w["w_conv_out"], col0=0, ncols=d, tm=tm, tn=1024, out_dtype=F32, mul=(gates, 0),
                 name="mm_conv_out")
    merged = _mm(hm, w["w_mlstm_out"], col0=0, ncols=d, tm=tm, tn=512, out_dtype=BF16, mul=(gates, d), add=conv_g,
                 name="mm_mlstm_out")
    x1 = _mm(merged, w["w_o"], col0=0, ncols=d, tm=tm, tn=1024, out_dtype=F32, res=(x, mod3, 2, grp_rows),
             name="mm_merge")
    h2 = _norm_mod(x1, w["norm2_g"], mod3, 3, 4, grp_rows, min(_NORM_ROWS, tm))
    pq = _mm(h2, w["peer_wq"], col0=0, ncols=w["peer_wq"].shape[1], tm=tm, tn=1024, out_dtype=F32, name="mm_peer_q")
    a_idx, b_idx, pg = _peer_topk(pq, w["peer_subkeys"], PEER_TOPK, tb=_PEER_TB)
    gmat = _peer_gates(a_idx, b_idx, pg, w["peer_subkeys"].shape[2], tg=LANES)
    return x1, h2, gmat


def kernel(x_prompt, x_sample, state_conv, state_mlstm_c, state_mlstm_n, state_mlstm_m, c_prompt, c_sample,
           ada_w, ada_b, norm1_g, norm2_g, w_in, mlstm_if_b, conv_dw, conv_db, conv_ln_g, conv_ln_b,
           w_conv_out, mlstm_norm_g, w_mlstm_out, w_o, peer_wq, peer_subkeys, peer_u, peer_v, final_g):
    bsz, seq, d = x_prompt.shape
    nb, dseq, _ = x_sample.shape
    assert dseq == 1
    assert ada_w.shape[0] == 1, "single-layer stack only"
    n_heads = mlstm_if_b.shape[1] // 2
    d_ml = mlstm_norm_g.shape[1]
    col_if = 2 * d + 4 * d_ml
    xp = x_prompt.reshape(bsz * seq, d)
    xs = x_sample.reshape(nb, d)
    n_c = nb + bsz
    c_all = jnp.pad(jnp.concatenate([c_sample, c_prompt], axis=0), ((0, (-n_c) % 16), (0, 0)))
    mod = _ada(c_all, ada_w[0], ada_b[0])
    mod_s = mod[:nb].reshape(1, nb, 6 * d)
    mod_p = mod[nb:n_c].reshape(bsz, 1, 6 * d)
    w_in_t = w_in[0].T
    w_if_t = jnp.pad(w_in_t[col_if:col_if + 2 * n_heads], ((0, LANES - 2 * n_heads), (0, 0)))
    w_gate_t = w_in_t[col_if + 2 * n_heads:]
    w = dict(norm1_g=norm1_g[0], norm2_g=norm2_g[0], w_in_t=w_in_t, w_if_t=w_if_t, w_gate_t=w_gate_t,
             w_conv_out=w_conv_out[0], mlstm_norm_g=mlstm_norm_g[0], w_mlstm_out=w_mlstm_out[0], w_o=w_o[0],
             peer_wq=peer_wq[0], peer_subkeys=peer_subkeys[0])
    conv_w = (conv_dw[0], conv_db[0], conv_ln_g[0], conv_ln_b[0])
    if_b, ml_g = mlstm_if_b[0], mlstm_norm_g[0]

    us, qkv_s, og_s, if_s, gates_s = _in_proj(xs, mod_s, nb, w, n_heads, F32)
    ys, buf_s_t = _conv_sample(state_conv[0].transpose(1, 0, 2), us, *conv_w)
    bs = buf_s_t.transpose(1, 0, 2)
    sample_step = (qkv_s, og_s, if_s, if_b, ml_g, state_mlstm_c[0], state_mlstm_n[0], state_mlstm_m[0], n_heads)

    up, qkv_p, og_p, if_p, gates_p, (u_tab, v_tab) = _in_proj(xp, mod_p, seq, w, n_heads, BF16,
                                                              tables=(peer_u[0], peer_v[0]))
    up3 = up.reshape(bsz, seq, d)
    yp = _conv_prompt(up3, *conv_w, tt=256).reshape(bsz * seq, d)
    bp = up3[:, seq - (conv_dw.shape[1] - 1):, :]
    hm_p, cp, np_, mp = _mlstm_prompt(qkv_p, og_p, if_p, if_b, ml_g, bsz, seq, n_heads, 256)
    xp, h2_p, gmat_p = _mix_and_route(xp, mod_p, seq, w, yp, hm_p, gates_p)

    n_steps = (bsz * seq // min(_PEER_TB, bsz * seq)) * (peer_subkeys.shape[3] // _PEER_KEYS_PER_STEP + 1)
    if n_steps >= n_heads * nb:
        pp, (hm_s, cs, ns, ms) = _peer_dense(h2_p, u_tab, v_tab, gmat_p, _PEER_TB, _PEER_KEYS_PER_STEP, sample_step)
    else:
        pp, _ = _peer_dense(h2_p, u_tab, v_tab, gmat_p, _PEER_TB, _PEER_KEYS_PER_STEP)
        hm_s, cs, ns, ms = _mlstm_sample(*sample_step)

    xs, h2_s, gmat_s = _mix_and_route(xs, mod_s, nb, w, ys, hm_s, gates_s)
    ps, _ = _peer_dense(h2_s, u_tab, v_tab, gmat_s, _PEER_TB, _G_KEYS)
    y_prompt = _final(xp, pp, mod_p, 5, final_g, seq, min(512, seq)).reshape(bsz, seq, d)
    y_sample = _final(xs, ps, mod_s, 5, final_g, nb, nb).reshape(nb, 1, d)
    return (y_prompt, y_sample) + tuple(o[None] for o in (bp, bs, cp, cs, np_, ns, mp, ms))
```

```python
import functools

import jax
import jax.numpy as jnp
from jax import lax
from jax.experimental import pallas as pl
from jax.experimental.pallas import tpu as pltpu

F32 = jnp.float32
BF16 = jnp.bfloat16
I32 = jnp.int32

EPS = 1e-6
NEG_INIT = -1e30
PEER_TOPK = 16
LANES = 128
VMEM_LIMIT = 56 * 1024 * 1024


def _cparams(*sem):
    return pltpu.CompilerParams(dimension_semantics=sem, vmem_limit_bytes=VMEM_LIMIT)


def _tile(target, *extents):
    t = min([target] + [e for e in extents if e])
    while any(e % t for e in extents):
        t -= LANES
    return t


def _sigmoid(x):
    return 1.0 / (1.0 + jnp.exp(-x))


def _log_sigmoid(x):
    return jnp.minimum(x, 0.0) - jnp.log1p(jnp.exp(-jnp.abs(x)))


def _split3(x):
    hi = x.astype(BF16)
    r1 = x - hi.astype(F32)
    mid = r1.astype(BF16)
    lo = (r1 - mid.astype(F32)).astype(BF16)
    return hi, mid, lo


def _ada_kernel(c_ref, w_ref, b_ref, o_ref):
    c = c_ref[...]
    a = (c * _sigmoid(c)).astype(BF16)
    o_ref[...] = jnp.dot(a, w_ref[...].astype(BF16), preferred_element_type=F32) + b_ref[...]


def _ada(c, w, b):
    m, d = c.shape
    n = w.shape[1]
    tn = _tile(1024, n)
    return pl.pallas_call(
        _ada_kernel,
        grid=(n // tn,),
        in_specs=[pl.BlockSpec((m, d), lambda j: (0, 0)),
                  pl.BlockSpec((d, tn), lambda j: (0, j)),
                  pl.BlockSpec((1, tn), lambda j: (0, j))],
        out_specs=pl.BlockSpec((m, tn), lambda j: (0, j)),
        out_shape=jax.ShapeDtypeStruct((m, n), F32),
        compiler_params=_cparams("arbitrary"),
        name="ada_mod",
    )(c, w, b.reshape(1, n))


def _norm_mod_kernel(x_ref, g_ref, sc_ref, sh_ref, o_ref):
    x = x_ref[...]
    y = x * lax.rsqrt(jnp.mean(x * x, axis=-1, keepdims=True) + EPS)
    o_ref[...] = (y * g_ref[...] * (1.0 + sc_ref[0]) + sh_ref[0]).astype(o_ref.dtype)


def _norm_mod(x, g, mod3, sh_idx, sc_idx, grp_rows, tm):
    t, d = x.shape
    r = mod3.shape[1]
    per = grp_rows // tm
    return pl.pallas_call(
        _norm_mod_kernel,
        grid=(t // tm,),
        in_specs=[pl.BlockSpec((tm, d), lambda i: (i, 0)),
                  pl.BlockSpec((1, d), lambda i: (0, 0)),
                  pl.BlockSpec((1, r, d), lambda i: (i // per, 0, sc_idx)),
                  pl.BlockSpec((1, r, d), lambda i: (i // per, 0, sh_idx))],
        out_specs=pl.BlockSpec((tm, d), lambda i: (i, 0)),
        out_shape=jax.ShapeDtypeStruct((t, d), BF16),
        compiler_params=_cparams("arbitrary"),
        name="norm_mod",
    )(x, g.reshape(1, d), mod3, mod3)


def _load_weight_tile(w_ref, transposed):
    w = w_ref[...]
    return (w.T if transposed else w).astype(BF16)


def _mm_kernel(*refs, act, has_mul, has_add, has_res, has_cast, w_transposed):
    a_ref, w_ref = refs[0], refs[1]
    wb_ref = refs[-1]
    if has_cast:
        cast_in_ref, o_ref, cast_out_ref = refs[-4], refs[-3], refs[-2]
        extra = list(refs[2:-4])
    else:
        o_ref = refs[-2]
        extra = list(refs[2:-2])

    @pl.when(pl.program_id(1) == 0)
    def _():
        wb_ref[...] = _load_weight_tile(w_ref, w_transposed)

    if has_cast:
        cast_out_ref[...] = cast_in_ref[...].astype(BF16)

    acc = jnp.dot(a_ref[...], wb_ref[...], preferred_element_type=F32)
    if act == "sigmoid":
        acc = _sigmoid(acc)
    if has_mul:
        acc = acc * extra.pop(0)[...]
    if has_add:
        acc = acc + extra.pop(0)[...]
    if has_res:
        x_ref, g_ref = extra
        acc = x_ref[...] + g_ref[0] * acc
    o_ref[...] = acc.astype(o_ref.dtype)


def _mm(a, w, *, col0, ncols, tm, tn, out_dtype, act=None, mul=None, add=None, res=None, w_transposed=False,
        cast=None, name="mm"):
    m, k = a.shape
    tm = min(tm, m)
    tn = _tile(tn, ncols, col0, mul[1] if mul else 0)
    cb = col0 // tn
    nj = ncols // tn
    w_spec = (pl.BlockSpec((tn, k), lambda j, i: (cb + j, 0)) if w_transposed
              else pl.BlockSpec((k, tn), lambda j, i: (0, cb + j)))
    in_specs = [pl.BlockSpec((tm, k), lambda j, i: (i, 0)), w_spec]
    operands = [a, w]
    tile = pl.BlockSpec((tm, tn), lambda j, i: (i, j))
    if mul is not None:
        mb = mul[1] // tn
        in_specs.append(pl.BlockSpec((tm, tn), lambda j, i: (i, mb + j)))
        operands.append(mul[0])
    if add is not None:
        in_specs.append(tile)
        operands.append(add)
    if res is not None:
        x, mod3, g_idx, grp_rows = res
        per = grp_rows // tm
        in_specs += [tile, pl.BlockSpec((1, mod3.shape[1], tn), lambda j, i: (i // per, 0, g_idx * nj + j))]
        operands += [x, mod3]
    ni = m // tm
    out_specs = [tile]
    out_shape = [jax.ShapeDtypeStruct((m, ncols), out_dtype)]
    if cast is not None:
        rows, cols = cast.shape
        rb = rows // (nj * ni)
        assert rb * nj * ni == rows and rb % 16 == 0
        cast_spec = pl.BlockSpec((rb, cols), lambda j, i: (j * ni + i, 0))
        in_specs.append(cast_spec)
        operands.append(cast)
        out_specs.append(cast_spec)
        out_shape.append(jax.ShapeDtypeStruct((rows, cols), BF16))
    outs = pl.pallas_call(
        functools.partial(_mm_kernel, act=act, has_mul=mul is not None, has_add=add is not None,
                          has_res=res is not None, has_cast=cast is not None, w_transposed=w_transposed),
        grid=(nj, ni),
        in_specs=in_specs,
        out_specs=out_specs,
        out_shape=out_shape,
        scratch_shapes=[pltpu.VMEM((k, tn), BF16)],
        compiler_params=_cparams("arbitrary", "arbitrary"),
        name=name,
    )(*operands)
    return outs[0] if cast is None else tuple(outs)


def _mm_glu_kernel(a_ref, wa_ref, wg_ref, o_ref, wab_ref, wgb_ref):
    @pl.when(pl.program_id(1) == 0)
    def _():
        wab_ref[...] = _load_weight_tile(wa_ref, True)
        wgb_ref[...] = _load_weight_tile(wg_ref, True)

    a = a_ref[...]
    lin = jnp.dot(a, wab_ref[...], preferred_element_type=F32)
    gate = jnp.dot(a, wgb_ref[...], preferred_element_type=F32)
    o_ref[...] = lin * _sigmoid(gate)


def _mm_glu(a, w_t, *, d_half, tm, tn):
    m, k = a.shape
    tm = min(tm, m)
    tn = _tile(tn, d_half)
    off = d_half // tn
    return pl.pallas_call(
        _mm_glu_kernel,
        grid=(d_half // tn, m // tm),
        in_specs=[pl.BlockSpec((tm, k), lambda j, i: (i, 0)),
                  pl.BlockSpec((tn, k), lambda j, i: (j, 0)),
                  pl.BlockSpec((tn, k), lambda j, i: (off + j, 0))],
        out_specs=pl.BlockSpec((tm, tn), lambda j, i: (i, j)),
        out_shape=jax.ShapeDtypeStruct((m, d_half), F32),
        scratch_shapes=[pltpu.VMEM((k, tn), BF16), pltpu.VMEM((k, tn), BF16)],
        compiler_params=_cparams("arbitrary", "arbitrary"),
        name="mm_glu",
    )(a, w_t, w_t)


_HALO = 32
_CONV_RC = 64


def _ln_silu(y, lg, lb):
    mu = jnp.mean(y, axis=-1, keepdims=True)
    yc = y - mu
    var = jnp.mean(yc * yc, axis=-1, keepdims=True)
    z = yc * lax.rsqrt(var + EPS) * lg + lb
    return z * _sigmoid(z)


def _conv_prompt_kernel(u_ref, dw_ref, db_ref, lg_ref, lb_ref, o_ref, win_ref, y_ref, sh_ref, *, tt, kw):
    d = u_ref.shape[2]
    t = pl.program_id(1)

    @pl.when(t == 0)
    def _():
        win_ref[0:_HALO, :] = jnp.zeros((_HALO, d), F32)

    @pl.when(t > 0)
    def _():
        win_ref[0:_HALO, :] = win_ref[tt:tt + _HALO, :]

    win_ref[_HALO:_HALO + tt, :] = u_ref[0]
    base = _HALO - (kw - 1)
    rc = min(_CONV_RC, tt)
    for mis in range(1, 8):
        sh_ref[mis - 1] = win_ref[mis:mis + tt + _HALO - 8, :]

    def row_chunk(r, carry):
        r0 = pl.multiple_of(r * rc, rc)
        for c in range(d // LANES):
            ls = slice(c * LANES, (c + 1) * LANES)
            acc = jnp.broadcast_to(db_ref[:, ls], (rc, LANES))
            for mis in range(8):
                taps = [j for j in range(kw) if (base + j) % 8 == mis]
                span = max(base + j - mis for j in taps) + rc
                w = win_ref[pl.ds(r0, span), ls] if mis == 0 else sh_ref[mis - 1, pl.ds(r0, span), ls]
                for j in taps:
                    off = base + j - mis
                    acc = acc + w[off:off + rc, :] * dw_ref[j:j + 1, ls]
            y_ref[pl.ds(r0, rc), ls] = acc
        return carry

    lax.fori_loop(0, tt // rc, row_chunk, 0)
    o_ref[0] = _ln_silu(y_ref[...], lg_ref[...], lb_ref[...]).astype(o_ref.dtype)


def _conv_prompt(u3, dw, db, lg, lb, tt):
    b, t, d = u3.shape
    tt = min(tt, t)
    kw = dw.shape[0]
    assert kw - 1 <= _HALO <= tt
    return pl.pallas_call(
        functools.partial(_conv_prompt_kernel, tt=tt, kw=kw),
        grid=(b, t // tt),
        in_specs=[pl.BlockSpec((1, tt, d), lambda i, j: (i, j, 0)),
                  pl.BlockSpec((kw, d), lambda i, j: (0, 0)),
                  pl.BlockSpec((1, d), lambda i, j: (0, 0)),
                  pl.BlockSpec((1, d), lambda i, j: (0, 0)),
                  pl.BlockSpec((1, d), lambda i, j: (0, 0))],
        out_specs=pl.BlockSpec((1, tt, d), lambda i, j: (i, j, 0)),
        out_shape=jax.ShapeDtypeStruct((b, t, d), BF16),
        scratch_shapes=[pltpu.VMEM((_HALO + tt, d), F32), pltpu.VMEM((tt, d), F32),
                        pltpu.VMEM((7, tt + _HALO - 8, d), F32)],
        compiler_params=_cparams("arbitrary", "arbitrary"),
        name="conv_prompt",
    )(u3, dw, db.reshape(1, d), lg.reshape(1, d), lb.reshape(1, d))


def _conv_sample_kernel(buf_ref, u_ref, dw_ref, db_ref, lg_ref, lb_ref, o_ref, nb_ref, *, kw):
    u = u_ref[...]
    acc = u * dw_ref[kw - 1:kw, :] + db_ref[...]
    for j in range(kw - 1):
        acc = acc + buf_ref[j] * dw_ref[j:j + 1, :]
    o_ref[...] = _ln_silu(acc, lg_ref[...], lb_ref[...]).astype(o_ref.dtype)
    for j in range(kw - 2):
        nb_ref[j] = buf_ref[j + 1]
    nb_ref[kw - 2] = u


def _conv_sample(buf_t, u, dw, db, lg, lb, bb=16):
    b, d = u.shape
    kw = dw.shape[0]
    bb = min(bb, b)
    return pl.pallas_call(
        functools.partial(_conv_sample_kernel, kw=kw),
        grid=(b // bb,),
        in_specs=[pl.BlockSpec((kw - 1, bb, d), lambda i: (0, i, 0)),
                  pl.BlockSpec((bb, d), lambda i: (i, 0)),
                  pl.BlockSpec((kw, d), lambda i: (0, 0)),
                  pl.BlockSpec((1, d), lambda i: (0, 0)),
                  pl.BlockSpec((1, d), lambda i: (0, 0)),
                  pl.BlockSpec((1, d), lambda i: (0, 0))],
        out_specs=[pl.BlockSpec((bb, d), lambda i: (i, 0)),
                   pl.BlockSpec((kw - 1, bb, d), lambda i: (0, i, 0))],
        out_shape=[jax.ShapeDtypeStruct((b, d), BF16),
                   jax.ShapeDtypeStruct((kw - 1, b, d), F32)],
        compiler_params=_cparams("arbitrary"),
        name="conv_sample",
    )(buf_t, u, dw, db.reshape(1, d), lg.reshape(1, d), lb.reshape(1, d))


_MLSTM_CB = 256


def _mlstm_prompt_kernel(bias_ref, q_ref, k_ref, v_ref, ifc_ref, ifr_ref, og_ref, ng_ref,
                         h_ref, c_ref, n_ref, m_ref, cb_ref, ns_ref, ms_ref, *, n_heads):
    hd = pl.program_id(1)
    ck = pl.program_id(2)
    L, dk = q_ref.shape
    dv = v_ref.shape[1]
    scale = dk ** -0.5

    @pl.when(ck == 0)
    def _():
        c_ref[...] = jnp.zeros(c_ref.shape, F32)
        cb_ref[...] = jnp.zeros(cb_ref.shape, BF16)
        ns_ref[...] = jnp.zeros(ns_ref.shape, F32)
        ms_ref[...] = jnp.full(ms_ref.shape, NEG_INIT, F32)

    q = q_ref[...]
    k = k_ref[...]
    v = v_ref[...]
    bi = bias_ref[hd]
    bf = bias_ref[n_heads + hd]
    col = ifc_ref[0, 0]
    li_c = col[:, 0:1] + bi
    lf_c = _log_sigmoid(col[:, 1:2] + bf)
    row = ifr_ref[0, 0]
    li_r = row[0:1, :] + bi
    lf_r = _log_sigmoid(row[1:2, :] + bf)

    r_io = lax.broadcasted_iota(I32, (L, L), 0)
    c_io = lax.broadcasted_iota(I32, (L, L), 1)
    causal = r_io >= c_io
    tri = jnp.where(causal, 1.0, 0.0).astype(BF16)
    tri_t = jnp.where(r_io <= c_io, 1.0, 0.0).astype(BF16)
    b_c = sum(jnp.dot(tri, p, preferred_element_type=F32)
              for p in _split3(jnp.broadcast_to(lf_c, (L, LANES))))[:, 0:1]
    b_r = sum(jnp.dot(p, tri_t, preferred_element_type=F32)
              for p in _split3(jnp.broadcast_to(lf_r, (16, L))))[0:1, :]

    m_prev = ms_ref[0:1, 0:1]
    b_last = b_c[L - 1:L, :]
    dmat = jnp.where(causal, b_c - (b_r - li_r), -jnp.inf)
    inter = b_c + m_prev
    m_t = jnp.maximum(inter, jnp.max(dmat, axis=-1, keepdims=True))
    w_inter = jnp.exp(inter - m_t)
    s = lax.dot_general(q, k, (((1,), (1,)), ((), ())), preferred_element_type=F32)
    s = s * scale * jnp.exp(dmat - m_t)
    qc = jnp.dot(q, cb_ref[...], preferred_element_type=F32)
    num = w_inter * qc + jnp.dot(s.astype(BF16), v, preferred_element_type=F32)
    qn = jnp.sum(q.astype(F32) * ns_ref[...], axis=-1, keepdims=True)
    den = w_inter * qn + jnp.sum(s, axis=-1, keepdims=True)
    h = num / jnp.maximum(jnp.abs(den), jnp.exp(-m_t))
    hn = h * lax.rsqrt(jnp.mean(h * h, axis=-1, keepdims=True) + EPS)
    h_ref[...] = (hn * ng_ref[...] * og_ref[...]).astype(h_ref.dtype)

    g_c = b_last - b_c + li_c
    m_new = jnp.maximum(b_last + m_prev, jnp.max(g_c, axis=0, keepdims=True))
    w_s = jnp.exp(g_c - m_new)
    decay = jnp.exp(b_last + m_prev - m_new)
    kw = k.astype(F32) * (w_s * scale)
    kwb = kw.astype(BF16)
    cbw = min(_MLSTM_CB, dv)
    for j in range(dv // cbw):
        cs = slice(j * cbw, (j + 1) * cbw)
        kv = lax.dot_general(kwb, v[:, cs], (((0,), (0,)), ((), ())), preferred_element_type=F32)
        c_new = decay * c_ref[0, 0, :, cs] + kv
        c_ref[0, 0, :, cs] = c_new
        cb_ref[:, cs] = c_new.astype(BF16)
    n_new = decay * ns_ref[...] + jnp.sum(kw, axis=0, keepdims=True)
    ns_ref[...] = n_new
    ms_ref[...] = jnp.broadcast_to(m_new, ms_ref.shape)
    n_ref[0, 0] = n_new
    m_ref[0, 0] = jnp.broadcast_to(m_new, (1, LANES))


def _mlstm_prompt(qkv, og, if_pre, if_b, norm_g, batch, seq, n_heads, chunk):
    dh = qkv.shape[1] // (3 * n_heads)
    chunk = min(chunk, seq)
    nc = seq // chunk
    ifx = if_pre.reshape(batch, seq, 2, n_heads)
    ifc = ifx.transpose(0, 3, 1, 2)
    ifr = ifx.transpose(0, 3, 2, 1)
    row = lambda b, h, c: (b * nc + c)
    h_out, c_out, n_out, m_out = pl.pallas_call(
        functools.partial(_mlstm_prompt_kernel, n_heads=n_heads),
        grid=(batch, n_heads, nc),
        in_specs=[pl.BlockSpec(memory_space=pltpu.SMEM),
                  pl.BlockSpec((chunk, dh), lambda b, h, c: (row(b, h, c), h)),
                  pl.BlockSpec((chunk, dh), lambda b, h, c: (row(b, h, c), n_heads + h)),
                  pl.BlockSpec((chunk, dh), lambda b, h, c: (row(b, h, c), 2 * n_heads + h)),
                  pl.BlockSpec((1, 1, chunk, 2), lambda b, h, c: (b, h, c, 0)),
                  pl.BlockSpec((1, 1, 2, chunk), lambda b, h, c: (b, h, 0, c)),
                  pl.BlockSpec((chunk, dh), lambda b, h, c: (row(b, h, c), h)),
                  pl.BlockSpec((1, dh), lambda b, h, c: (0, h))],
        out_specs=[pl.BlockSpec((chunk, dh), lambda b, h, c: (row(b, h, c), h)),
                   pl.BlockSpec((1, 1, dh, dh), lambda b, h, c: (b, h, 0, 0)),
                   pl.BlockSpec((1, 1, 1, dh), lambda b, h, c: (b, h, 0, 0)),
                   pl.BlockSpec((1, 1, 1, LANES), lambda b, h, c: (b, h, 0, 0))],
        out_shape=[jax.ShapeDtypeStruct((batch * seq, n_heads * dh), BF16),
                   jax.ShapeDtypeStruct((batch, n_heads, dh, dh), F32),
                   jax.ShapeDtypeStruct((batch, n_heads, 1, dh), F32),
                   jax.ShapeDtypeStruct((batch, n_heads, 1, LANES), F32)],
        scratch_shapes=[pltpu.VMEM((dh, dh), BF16), pltpu.VMEM((1, dh), F32), pltpu.VMEM((8, LANES), F32)],
        compiler_params=_cparams("arbitrary", "arbitrary", "arbitrary"),
        name="mlstm_prompt",
    )(if_b, qkv, qkv, qkv, ifc, ifr, og, norm_g.reshape(1, -1))
    return h_out, c_out, n_out[:, :, 0, :], m_out[:, :, 0, 0]


def _mlstm_sample_head_setup(b, q_ref, k_ref, qt_ref, kt_ref):
    @pl.when(b == 0)
    def _():
        for src, dst in ((q_ref, qt_ref), (k_ref, kt_ref)):
            for i, p in enumerate(_split3(src[...].T)):
                dst[i] = p


def _mlstm_sample_columns(b, qt_ref, kt_ref):
    nb = qt_ref.shape[2]
    sel = jnp.where(lax.broadcasted_iota(I32, (nb, LANES), 0) == b, 1.0, 0.0).astype(BF16)
    return tuple(sum(jnp.dot(t_ref[i], sel, preferred_element_type=F32) for i in range(3))
                 for t_ref in (qt_ref, kt_ref))


def _mlstm_sample_item(hd, b, columns, bias_ref, ifp_ref, mst_ref, q_ref, k_ref, v_ref, c_ref, n_ref, og_ref,
                       ng_ref, h_ref, co_ref, no_ref, mo_ref, qt_ref, kt_ref, *, n_heads):
    nb, dk = q_ref.shape
    dv = v_ref.shape[1]
    scale = dk ** -0.5
    qb, kb = columns
    li = jnp.full((1, 1), ifp_ref[b, hd] + bias_ref[hd], F32)
    lf = _log_sigmoid(jnp.full((1, 1), ifp_ref[b, n_heads + hd] + bias_ref[n_heads + hd], F32))
    m_prev = jnp.full((1, 1), mst_ref[b, hd], F32)
    m_t = jnp.maximum(lf + m_prev, li)
    decay = jnp.exp(lf + m_prev - m_t)
    w_s = jnp.exp(li - m_t)

    q_row = q_ref[pl.ds(b, 1), :]
    k_row = k_ref[pl.ds(b, 1), :]
    v_row = v_ref[pl.ds(b, 1), :]
    n_row = n_ref[0, 0]
    sval = jnp.sum(q_row * k_row, axis=-1, keepdims=True) * scale * w_s
    den = decay * jnp.sum(q_row * n_row, axis=-1, keepdims=True) + sval
    denom = jnp.maximum(jnp.abs(den), jnp.exp(-m_t))
    wk = w_s * scale
    pieces = []
    for j in range(dv // LANES):
        ls = slice(j * LANES, (j + 1) * LANES)
        cj = c_ref[0, 0, :, ls]
        vj = v_row[:, ls]
        qc = jnp.sum(qb * cj, axis=0, keepdims=True)
        co_ref[0, 0, :, ls] = decay * cj + kb * (wk * vj)
        pieces.append((decay * qc + sval * vj) / denom)
    h = jnp.concatenate(pieces, axis=1)
    hn = h * lax.rsqrt(jnp.mean(h * h, axis=-1, keepdims=True) + EPS)
    h_ref[0] = (hn * ng_ref[...] * og_ref[0]).astype(h_ref.dtype)
    no_ref[0, 0] = decay * n_row + wk * k_row
    mo_ref[0, 0] = jnp.broadcast_to(m_t, (1, LANES))


def _mlstm_sample_kernel(*refs, n_heads):
    b = pl.program_id(1)
    _mlstm_sample_head_setup(b, refs[3], refs[4], refs[-2], refs[-1])
    _mlstm_sample_item(pl.program_id(0), b, _mlstm_sample_columns(b, refs[-2], refs[-1]), *refs, n_heads=n_heads)


def _mlstm_sample_plumbing(qkv, og, if_pre, if_b, norm_g, st_c, st_n, st_m, n_heads, item):
    nb = qkv.shape[0]
    dh = qkv.shape[1] // (3 * n_heads)

    def at(fn):
        return lambda *ids: fn(*item(*ids))

    smem = pl.BlockSpec(memory_space=pltpu.SMEM)
    in_specs = [smem, smem, smem,
                pl.BlockSpec((nb, dh), at(lambda h, b: (0, h))),
                pl.BlockSpec((nb, dh), at(lambda h, b: (0, n_heads + h))),
                pl.BlockSpec((nb, dh), at(lambda h, b: (0, 2 * n_heads + h))),
                pl.BlockSpec((1, 1, dh, dh), at(lambda h, b: (b, h, 0, 0))),
                pl.BlockSpec((1, 1, 1, dh), at(lambda h, b: (b, h, 0, 0))),
                pl.BlockSpec((1, 1, dh), at(lambda h, b: (b, 0, h))),
                pl.BlockSpec((1, dh), at(lambda h, b: (0, h)))]
    operands = [if_b, if_pre, st_m, qkv, qkv, qkv, st_c, st_n.reshape(nb, n_heads, 1, dh),
                og.reshape(nb, 1, n_heads * dh), norm_g.reshape(1, -1)]
    out_specs = [pl.BlockSpec((1, 1, dh), at(lambda h, b: (b, 0, h))),
                 pl.BlockSpec((1, 1, dh, dh), at(lambda h, b: (b, h, 0, 0))),
                 pl.BlockSpec((1, 1, 1, dh), at(lambda h, b: (b, h, 0, 0))),
                 pl.BlockSpec((1, 1, 1, LANES), at(lambda h, b: (b, h, 0, 0)))]
    out_shape = [jax.ShapeDtypeStruct((nb, 1, n_heads * dh), BF16),
                 jax.ShapeDtypeStruct((nb, n_heads, dh, dh), F32),
                 jax.ShapeDtypeStruct((nb, n_heads, 1, dh), F32),
                 jax.ShapeDtypeStruct((nb, n_heads, 1, LANES), F32)]
    scratch = [pltpu.VMEM((3, dh, nb), BF16), pltpu.VMEM((3, dh, nb), BF16)]
    return operands, in_specs, out_specs, out_shape, scratch


def _mlstm_sample_results(h_out, c_out, n_out, m_out):
    return h_out.reshape(h_out.shape[0], -1), c_out, n_out[:, :, 0, :], m_out[:, :, 0, 0]


def _mlstm_sample(qkv, og, if_pre, if_b, norm_g, st_c, st_n, st_m, n_heads):
    operands, in_specs, out_specs, out_shape, scratch = _mlstm_sample_plumbing(
        qkv, og, if_pre, if_b, norm_g, st_c, st_n, st_m, n_heads, lambda h, b: (h, b))
    outs = pl.pallas_call(
        functools.partial(_mlstm_sample_kernel, n_heads=n_heads),
        grid=(n_heads, qkv.shape[0]),
        in_specs=in_specs, out_specs=out_specs, out_shape=out_shape, scratch_shapes=scratch,
        compiler_params=_cparams("arbitrary", "arbitrary"),
        name="mlstm_sample",
    )(*operands)
    return _mlstm_sample_results(*outs)


def _topk_rows(problems, n_take):
    state = []
    for s, payload in problems:
        iota = lax.broadcasted_iota(I32, s.shape, 0).astype(F32)
        state.append([s, iota, payload, [], []])
    for _ in range(n_take):
        for st in state:
            s, iota, payload, vals, picks = st
            m = jnp.max(s, axis=0, keepdims=True)
            pos = jnp.min(jnp.where(s == m, iota, float(s.shape[0])), axis=0, keepdims=True)
            hit = iota == pos
            vals.append(m)
            picks.append(pos if payload is None
                         else jnp.max(jnp.where(hit, payload, -1.0), axis=0, keepdims=True))
            st[0] = jnp.where(hit, -jnp.inf, s)
    return [(st[3], st[4]) for st in state]


def _peer_pairs(topk):
    return [(i, j) for i in range(topk) for j in range(topk) if (i + 1) * (j + 1) <= topk]


def _peer_topk_kernel(q_ref, sk_ref, a_ref, b_ref, g_ref, cv_ref, ce_ref, *, topk):
    tb = q_ref.shape[0]
    nk, dsub = sk_ref.shape[2], sk_ref.shape[3]
    pairs = _peer_pairs(topk)
    sk = [sk_ref[0, p].astype(BF16) for p in range(2)]

    n_groups = tb // LANES
    per_iter = cv_ref.shape[0]

    def groups(gi, carry):
        starts = [pl.multiple_of((gi * per_iter + n) * LANES, LANES) for n in range(per_iter)]
        for n, t0 in enumerate(starts):
            scores = []
            for p in range(2):
                qp = q_ref[pl.ds(t0, LANES), p * dsub:(p + 1) * dsub].astype(BF16)
                scores.append(lax.dot_general(sk[p], qp, (((1,), (1,)), ((), ())),
                                              preferred_element_type=F32))
            (sv0, si0), (sv1, si1) = _topk_rows([(scores[0], None), (scores[1], None)], topk)
            cv_ref[n] = jnp.full(cv_ref.shape[1:], -jnp.inf, F32)
            ce_ref[n] = jnp.zeros(ce_ref.shape[1:], F32)
            for r, (i, j) in enumerate(pairs):
                cv_ref[n, r:r + 1, :] = sv0[i] + sv1[j]
                ce_ref[n, r:r + 1, :] = si0[i] * float(nk) + si1[j]
        picked = _topk_rows([(cv_ref[n], ce_ref[n]) for n in range(per_iter)], topk)
        for t0, (tv, te) in zip(starts, picked):
            ex = [jnp.exp(x - tv[0]) for x in tv]
            tot = sum(ex)
            for r in range(topk):
                first = jnp.floor(te[r] * (1.0 / nk))
                a_ref[r:r + 1, pl.ds(t0, LANES)] = first.astype(I32)
                b_ref[r:r + 1, pl.ds(t0, LANES)] = (te[r] - first * float(nk)).astype(I32)
                g_ref[r:r + 1, pl.ds(t0, LANES)] = ex[r] / tot
        return carry

    lax.fori_loop(0, n_groups // per_iter, groups, 0)


def _peer_topk(q, subkeys, topk, tb):
    t = q.shape[0]
    nh, _, nk, dsub = subkeys.shape
    tb = min(tb, t)
    assert nk & (nk - 1) == 0 and nk * nk < 2 ** 24
    n_pad = -(-len(_peer_pairs(topk)) // 8) * 8
    per_iter = max(n for n in (4, 2, 1) if (tb // LANES) % n == 0)
    out = jax.ShapeDtypeStruct((nh * topk, t), I32)
    return pl.pallas_call(
        functools.partial(_peer_topk_kernel, topk=topk),
        grid=(t // tb, nh),
        in_specs=[pl.BlockSpec((tb, 2 * dsub), lambda i, h: (i, h)),
                  pl.BlockSpec((1, 2, nk, dsub), lambda i, h: (h, 0, 0, 0))],
        out_specs=[pl.BlockSpec((topk, tb), lambda i, h: (h, i))] * 3,
        out_shape=[out, out, jax.ShapeDtypeStruct((nh * topk, t), F32)],
        scratch_shapes=[pltpu.VMEM((per_iter, n_pad, LANES), F32)] * 2,
        compiler_params=_cparams("arbitrary", "arbitrary"),
        name="peer_topk",
    )(q, subkeys)


def _peer_gates_kernel(a_ref, b_ref, g_ref, o_ref, at_ref, bt_ref, gt_ref):
    nk = o_ref.shape[1]
    tg = o_ref.shape[0]
    at_ref[...] = a_ref[...].T
    bt_ref[...] = b_ref[...].T
    gt_ref[...] = g_ref[...].T
    key = lax.broadcasted_iota(I32, (nk, at_ref.shape[1]), 0)

    def token(t, carry):
        a = at_ref[pl.ds(t, 1), :]
        b = bt_ref[pl.ds(t, 1), :]
        g = gt_ref[pl.ds(t, 1), :]
        first = jnp.where(key == a, g, 0.0).astype(BF16)
        second = jnp.where(key == b, 1.0, 0.0).astype(BF16)
        o_ref[t] = lax.dot_general(first, second, (((1,), (1,)), ((), ())), preferred_element_type=F32)
        return carry

    lax.fori_loop(0, tg, token, 0, unroll=32)


def _peer_gates(a_idx, b_idx, gates, nk, tg):
    hk, t = a_idx.shape
    tg = min(tg, t)
    return pl.pallas_call(
        _peer_gates_kernel,
        grid=(t // tg,),
        in_specs=[pl.BlockSpec((hk, tg), lambda i: (0, i))] * 3,
        out_specs=pl.BlockSpec((tg, nk, nk), lambda i: (i, 0, 0)),
        out_shape=jax.ShapeDtypeStruct((t, nk, nk), F32),
        scratch_shapes=[pltpu.VMEM((tg, hk), I32), pltpu.VMEM((tg, hk), I32), pltpu.VMEM((tg, hk), F32)],
        compiler_params=_cparams("arbitrary"),
        name="peer_gates",
    )(a_idx, b_idx, gates)


_SQRT_HALF = 0.7071067811865476


_G_KEYS = 8
_N_RIDER_IN = 10
_N_RIDER_OUT = 4


def _peer_dense_kernel(*refs, n_blocks, keys_per_step, rider):
    x_ref, u_ref, v_ref, g_ref = refs[:4]
    if rider is None:
        o_ref, coef_ref = refs[4:]
    else:
        rider_in = refs[4:4 + _N_RIDER_IN]
        o_ref = refs[4 + _N_RIDER_IN]
        rider_out = refs[5 + _N_RIDER_IN:5 + _N_RIDER_IN + _N_RIDER_OUT]
        coef_ref = refs[5 + _N_RIDER_IN + _N_RIDER_OUT]
        rider_scratch = refs[6 + _N_RIDER_IN + _N_RIDER_OUT:]
    e = pl.program_id(1)
    cur = e % 2
    prev = 1 - cur

    @pl.when(e == 0)
    def _():
        o_ref[...] = jnp.zeros(o_ref.shape, F32)
        coef_ref[1] = jnp.zeros(coef_ref.shape[1:], BF16)

    if rider is not None:
        n_heads, n_items, nb = rider
        item = jnp.minimum(pl.program_id(0) * (n_blocks + 1) + e, n_items - 1)
        _mlstm_sample_head_setup(item % nb, rider_in[3], rider_in[4], *rider_scratch)
        columns = _mlstm_sample_columns(item % nb, *rider_scratch)

    a = lax.dot_general(x_ref[...], u_ref[...], (((1,), (1,)), ((), ())), preferred_element_type=F32)
    o_ref[...] += jnp.dot(coef_ref[prev], v_ref[...], preferred_element_type=F32)
    if rider is not None:
        _mlstm_sample_item(item // nb, item % nb, columns, *rider_in, *rider_out, *rider_scratch, n_heads=n_heads)
    act = 0.5 * a * (1.0 + lax.erf(a * _SQRT_HALF))
    per_block = g_ref.shape[1] // keys_per_step
    first = 0 if per_block == 1 else (jnp.minimum(e, n_blocks - 1) % per_block) * keys_per_step
    g = jnp.concatenate([g_ref[:, first + s, :] for s in range(keys_per_step)], axis=1)
    coef_ref[cur] = (g * act).astype(BF16)


def _peer_dense(x, u_tab, v_tab, gmat, tb, keys_per_step, rider_args=None):
    t, d = x.shape
    nk = gmat.shape[1]
    tb = min(tb, t)
    eb = keys_per_step * nk
    ne = nk // keys_per_step
    per_block = _G_KEYS // keys_per_step
    cur_blk = lambda e: jnp.minimum(e, ne - 1)
    once = pl.Buffered(1)
    in_specs = [pl.BlockSpec((tb, d), lambda i, e: (i, 0), pipeline_mode=once),
                pl.BlockSpec((eb, d), lambda i, e: (cur_blk(e), 0)),
                pl.BlockSpec((eb, d), lambda i, e: (jnp.maximum(e - 1, 0), 0)),
                pl.BlockSpec((tb, _G_KEYS, nk), lambda i, e: (i, cur_blk(e) // per_block, 0))]
    operands = [x, u_tab, v_tab, gmat]
    out_specs = [pl.BlockSpec((tb, d), lambda i, e: (i, 0), pipeline_mode=once)]
    out_shape = [jax.ShapeDtypeStruct((t, d), F32)]
    scratch = [pltpu.VMEM((2, tb, eb), BF16)]
    rider = None
    if rider_args is not None:
        n_heads = rider_args[-1]
        nb = rider_args[0].shape[0]
        n_items = n_heads * nb
        assert (t // tb) * (ne + 1) >= n_items

        def item(i, e):
            k = jnp.minimum(i * (ne + 1) + e, n_items - 1)
            return k // nb, k % nb

        r_ops, r_in, r_out, r_shape, r_scratch = _mlstm_sample_plumbing(*rider_args, item)
        assert len(r_ops) == _N_RIDER_IN and len(r_out) == _N_RIDER_OUT
        operands += r_ops
        in_specs += r_in
        out_specs += r_out
        out_shape += r_shape
        scratch += r_scratch
        rider = (n_heads, n_items, nb)
    outs = pl.pallas_call(
        functools.partial(_peer_dense_kernel, n_blocks=ne, keys_per_step=keys_per_step, rider=rider),
        grid=(t // tb, ne + 1),
        in_specs=in_specs, out_specs=out_specs, out_shape=out_shape, scratch_shapes=scratch,
        compiler_params=_cparams("arbitrary", "arbitrary"),
        name="peer_dense" if rider is None else "peer_dense_mlstm_sample",
    )(*operands)
    return outs[0], (None if rider is None else _mlstm_sample_results(*outs[1:]))


def _final_kernel(x_ref, p_ref, g_ref, fg_ref, o_ref):
    x = x_ref[...] + g_ref[0] * p_ref[...]
    o_ref[...] = x * lax.rsqrt(jnp.mean(x * x, axis=-1, keepdims=True) + EPS) * fg_ref[...]


def _final(x, p, mod3, g_idx, final_g, grp_rows, tm):
    t, d = x.shape
    r = mod3.shape[1]
    per = grp_rows // tm
    return pl.pallas_call(
        _final_kernel,
        grid=(t // tm,),
        in_specs=[pl.BlockSpec((tm, d), lambda i: (i, 0)),
                  pl.BlockSpec((tm, d), lambda i: (i, 0)),
                  pl.BlockSpec((1, r, d), lambda i: (i // per, 0, g_idx)),
                  pl.BlockSpec((1, d), lambda i: (0, 0))],
        out_specs=pl.BlockSpec((tm, d), lambda i: (i, 0)),
        out_shape=jax.ShapeDtypeStruct((t, d), F32),
        compiler_params=_cparams("arbitrary"),
        name="final_norm",
    )(x, p, mod3, final_g.reshape(1, d))


_MM_ROWS = 1024
_NORM_ROWS = 512
_PEER_TB = 1024
_PEER_KEYS_PER_STEP = 2


def _in_proj(x, mod3, grp_rows, w, n_heads, act_dtype, tables=None):
    t, d = x.shape
    d_ml = w["mlstm_norm_g"].shape[0]
    tm = min(_MM_ROWS, grp_rows, t)
    h1 = _norm_mod(x, w["norm1_g"], mod3, 0, 1, grp_rows, min(_NORM_ROWS, tm))
    u = _mm_glu(h1, w["w_in_t"], d_half=d, tm=tm, tn=512)
    proj = functools.partial(_mm, h1, tm=tm, w_transposed=True)
    qkv = proj(w["w_in_t"], col0=2 * d, ncols=3 * d_ml, tn=1024, out_dtype=act_dtype, name="mm_qkv")
    og = proj(w["w_in_t"], col0=2 * d + 3 * d_ml, ncols=d_ml, tn=1024, out_dtype=act_dtype, act="sigmoid",
              cast=tables and tables[0], name="mm_ogate")
    if_pre = proj(w["w_if_t"], col0=0, ncols=LANES, tn=LANES, out_dtype=F32, name="mm_if")[:, :2 * n_heads]
    gates = proj(w["w_gate_t"], col0=0, ncols=2 * d, tn=1024, out_dtype=act_dtype, act="sigmoid",
                 cast=tables and tables[1], name="mm_gates")
    if tables is None:
        return u, qkv, og, if_pre, gates
    return u, qkv, og[0], if_pre, gates[0], (og[1], gates[1])


def _mix_and_route(x, mod3, grp_rows, w, y, hm, gates):
    t, d = x.shape
    tm = min(_MM_ROWS, grp_rows, t)
    conv_g = _mm(y, w["w_conv_out"], col0=0, ncols=d, tm=tm, tn=1024, out_dtype=gates.dtype, mul=(gates, 0),
                 name="mm_conv_out")
    merged = _mm(hm, w["w_mlstm_out"], col0=0, ncols=d, tm=tm, tn=512, out_dtype=BF16, mul=(gates, d), add=conv_g,
                 name="mm_mlstm_out")
    x1 = _mm(merged, w["w_o"], col0=0, ncols=d, tm=tm, tn=1024, out_dtype=F32, res=(x, mod3, 2, grp_rows),
             name="mm_merge")
    h2 = _norm_mod(x1, w["norm2_g"], mod3, 3, 4, grp_rows, min(_NORM_ROWS, tm))
    pq = _mm(h2, w["peer_wq"], col0=0, ncols=w["peer_wq"].shape[1], tm=tm, tn=1024, out_dtype=BF16, name="mm_peer_q")
    a_idx, b_idx, pg = _peer_topk(pq, w["peer_subkeys"], PEER_TOPK, tb=_PEER_TB)
    gmat = _peer_gates(a_idx, b_idx, pg, w["peer_subkeys"].shape[2], tg=LANES)
    return x1, h2, gmat


def kernel(x_prompt, x_sample, state_conv, state_mlstm_c, state_mlstm_n, state_mlstm_m, c_prompt, c_sample,
           ada_w, ada_b, norm1_g, norm2_g, w_in, mlstm_if_b, conv_dw, conv_db, conv_ln_g, conv_ln_b,
           w_conv_out, mlstm_norm_g, w_mlstm_out, w_o, peer_wq, peer_subkeys, peer_u, peer_v, final_g):
    bsz, seq, d = x_prompt.shape
    nb, dseq, _ = x_sample.shape
    assert dseq == 1
    assert ada_w.shape[0] == 1, "single-layer stack only"
    n_heads = mlstm_if_b.shape[1] // 2
    d_ml = mlstm_norm_g.shape[1]
    col_if = 2 * d + 4 * d_ml
    xp = x_prompt.reshape(bsz * seq, d)
    xs = x_sample.reshape(nb, d)
    n_c = nb + bsz
    c_all = jnp.pad(jnp.concatenate([c_sample, c_prompt], axis=0), ((0, (-n_c) % 16), (0, 0)))
    mod = _ada(c_all, ada_w[0], ada_b[0])
    mod_s = mod[:nb].reshape(1, nb, 6 * d)
    mod_p = mod[nb:n_c].reshape(bsz, 1, 6 * d)
    w_in_t = w_in[0].T
    w_if_t = jnp.pad(w_in_t[col_if:col_if + 2 * n_heads], ((0, LANES - 2 * n_heads), (0, 0)))
    w_gate_t = w_in_t[col_if + 2 * n_heads:]
    w = dict(norm1_g=norm1_g[0], norm2_g=norm2_g[0], w_in_t=w_in_t, w_if_t=w_if_t, w_gate_t=w_gate_t,
             w_conv_out=w_conv_out[0], mlstm_norm_g=mlstm_norm_g[0], w_mlstm_out=w_mlstm_out[0], w_o=w_o[0],
             peer_wq=peer_wq[0], peer_subkeys=peer_subkeys[0])
    conv_w = (conv_dw[0], conv_db[0], conv_ln_g[0], conv_ln_b[0])
    if_b, ml_g = mlstm_if_b[0], mlstm_norm_g[0]

    us, qkv_s, og_s, if_s, gates_s = _in_proj(xs, mod_s, nb, w, n_heads, F32)
    ys, buf_s_t = _conv_sample(state_conv[0].transpose(1, 0, 2), us, *conv_w)
    bs = buf_s_t.transpose(1, 0, 2)
    sample_step = (qkv_s, og_s, if_s, if_b, ml_g, state_mlstm_c[0], state_mlstm_n[0], state_mlstm_m[0], n_heads)

    up, qkv_p, og_p, if_p, gates_p, (u_tab, v_tab) = _in_proj(xp, mod_p, seq, w, n_heads, BF16,
                                                              tables=(peer_u[0], peer_v[0]))
    up3 = up.reshape(bsz, seq, d)
    yp = _conv_prompt(up3, *conv_w, tt=256).reshape(bsz * seq, d)
    bp = up3[:, seq - (conv_dw.shape[1] - 1):, :]
    hm_p, cp, np_, mp = _mlstm_prompt(qkv_p, og_p, if_p, if_b, ml_g, bsz, seq, n_heads, 256)
    xp, h2_p, gmat_p = _mix_and_route(xp, mod_p, seq, w, yp, hm_p, gates_p)

    n_steps = (bsz * seq // min(_PEER_TB, bsz * seq)) * (peer_subkeys.shape[3] // _PEER_KEYS_PER_STEP + 1)
    if n_steps >= n_heads * nb:
        pp, (hm_s, cs, ns, ms) = _peer_dense(h2_p, u_tab, v_tab, gmat_p, _PEER_TB, _PEER_KEYS_PER_STEP, sample_step)
    else:
        pp, _ = _peer_dense(h2_p, u_tab, v_tab, gmat_p, _PEER_TB, _PEER_KEYS_PER_STEP)
        hm_s, cs, ns, ms = _mlstm_sample(*sample_step)

    xs, h2_s, gmat_s = _mix_and_route(xs, mod_s, nb, w, ys, hm_s, gates_s)
    ps, _ = _peer_dense(h2_s, u_tab, v_tab, gmat_s, _PEER_TB, _G_KEYS)
    y_prompt = _final(xp, pp, mod_p, 5, final_g, seq, min(512, seq)).reshape(bsz, seq, d)
    y_sample = _final(xs, ps, mod_s, 5, final_g, nb, nb).reshape(nb, 1, d)
    return (y_prompt, y_sample) + tuple(o[None] for o in (bp, bs, cp, cs, np_, ns, mp, ms))
```
